```python
import math, functools
import jax, jax.numpy as jnp
from jax import lax
import numpy as np

D_MODEL = 2048
BATCH = 8
SEQ = 2048
DEPTH = 1
DEC_BATCH = 32
DEC_SEQ = 8
PAST_LEN = 16384
PAGE_SIZE = 128

ATT_HEADS = 8
HEAD_DIM = 128
D_ATT = ATT_HEADS * HEAD_DIM
D_CONV = D_MODEL - D_ATT
CONV_WIDTH = 3
DIL_PAIRS = ((128, 1), (512, 4), (2048, 16))
ATT_WINDOW = max(w for w, _ in DIL_PAIRS)
BAND_BLOCK = 128
ROPE_THETA = 10000.0
PEER_HEADS = 8
N_SUBKEYS = 128
N_EXPERTS = N_SUBKEYS * N_SUBKEYS
PEER_KEY_DIM = 256
PEER_TOPK = 16
PEER_TOKEN_BLOCK = 128
PLE_DIM = 256
DEEPNORM_ALPHA = (2.0 * DEPTH) ** 0.25
DEEPNORM_BETA = (8.0 * DEPTH) ** -0.25
LN_EPS = 1e-5

kernel_name = "hymba_dilated_conv_peer_decoder_step"


def layer_norm(x, g, b):
    xf = x.astype(jnp.float32)
    mu = jnp.mean(xf, -1, keepdims=True)
    var = jnp.mean(jnp.square(xf - mu), -1, keepdims=True)
    return ((xf - mu) * lax.rsqrt(var + LN_EPS) * g + b).astype(x.dtype)


def rope(x, pos):
    half = HEAD_DIM // 2
    inv = ROPE_THETA ** (-jnp.arange(half, dtype=jnp.float32) / half)
    ang = pos.astype(jnp.float32)[:, None] * inv[None, :]
    cos = jnp.cos(ang)[None, :, None, :]
    sin = jnp.sin(ang)[None, :, None, :]
    xf = x.astype(jnp.float32)
    x1, x2 = xf[..., :half], xf[..., half:]
    return jnp.concatenate([x1 * cos - x2 * sin, x2 * cos + x1 * sin], -1).astype(x.dtype)


def split_projection(x, w_in):
    h = jnp.einsum('btd,de->bte', x, w_in)
    q, k, v, u, gb, gc = jnp.split(
        h, [D_ATT, 2 * D_ATT, 3 * D_ATT, 3 * D_ATT + D_CONV, 3 * D_ATT + 2 * D_CONV], axis=-1)
    B, T = x.shape[:2]
    heads = lambda a: a.reshape(B, T, ATT_HEADS, HEAD_DIM)
    return heads(q), heads(k), heads(v), gc * u, gb


def short_conv(z_ext, conv_w, T):
    return sum(conv_w[j] * z_ext[:, j:j + T] for j in range(CONV_WIDTH))


def dilated_branch_prompt(q, k, v, sub_w, d):
    B, T, H, Dh = q.shape
    L = T // d
    nb = -(-L // BAND_BLOCK)
    Lp = nb * BAND_BLOCK
    G = B * d

    def to_sub(a):
        return a.reshape(B, L, d, H, Dh).transpose(0, 2, 1, 3, 4).reshape(G, L, H, Dh)

    qs = jnp.pad(to_sub(q), ((0, 0), (0, Lp - L), (0, 0), (0, 0))).reshape(G, nb, BAND_BLOCK, H, Dh)

    def band_keys(a):
        a = jnp.pad(to_sub(a), ((0, 0), (BAND_BLOCK, Lp - L), (0, 0), (0, 0)))
        a = a.reshape(G, nb + 1, BAND_BLOCK, H, Dh)
        return jnp.concatenate([a[:, :-1], a[:, 1:]], axis=2)

    ks, vs = band_keys(k), band_keys(v)
    s = jnp.einsum('gnqhd,gnkhd->gnqhk', qs, ks,
                   preferred_element_type=jnp.float32) * (HEAD_DIM ** -0.5)
    qi = jnp.arange(BAND_BLOCK)[:, None]
    kj = jnp.arange(2 * BAND_BLOCK)[None, :]
    dist = BAND_BLOCK + qi - kj
    kpos = (jnp.arange(nb)[:, None] - 1) * BAND_BLOCK + jnp.arange(2 * BAND_BLOCK)[None, :]
    valid = ((dist >= 0) & (dist <= sub_w))[None] & (kpos >= 0)[:, None, :]
    s = jnp.where(valid[None, :, :, None, :], s, -jnp.inf)
    m = jnp.max(s, axis=-1)
    p = jnp.exp(s - m[..., None])
    l = jnp.sum(p, axis=-1)
    acc = jnp.einsum('gnqhk,gnkhd->gnqhd', p, vs.astype(jnp.float32))

    def from_sub(a):
        tail = a.shape[3:]
        a = a.reshape((B, d, Lp) + tail)[:, :, :L]
        return jnp.moveaxis(a, 1, 2).reshape((B, T) + tail)

    return from_sub(acc), from_sub(m), from_sub(l)


def dilated_branch_sample(q, k_all, v_all, sub_w, d):
    S = q.shape[1]
    n_buf = k_all.shape[1] - S
    idx = n_buf + jnp.arange(S)[:, None] - d * jnp.arange(sub_w + 1)[None, :]
    valid = idx >= 0
    idx = jnp.maximum(idx, 0)
    kg = k_all[:, idx]
    vg = v_all[:, idx]
    s = jnp.einsum('bshd,bsjhd->bshj', q, kg,
                   preferred_element_type=jnp.float32) * (HEAD_DIM ** -0.5)
    s = jnp.where(valid[None, :, None, :], s, -jnp.inf)
    m = jnp.max(s, axis=-1)
    p = jnp.exp(s - m[..., None])
    l = jnp.sum(p, axis=-1)
    acc = jnp.einsum('bshj,bsjhd->bshd', p, vg.astype(jnp.float32))
    return acc, m, l


def combine_branches(parts):
    big_m = functools.reduce(jnp.maximum, [m for _, m, _ in parts])
    num = sum(jnp.exp(m - big_m)[..., None] * acc for acc, m, _ in parts)
    den = sum(jnp.exp(m - big_m) * l for _, m, l in parts)
    return num / den[..., None]


def peer_ffn(x, w_q, sub_keys, u_tab, v_tab):
    B, T, D = x.shape
    N = B * T
    xf = x.reshape(N, D)
    q = jnp.einsum('nd,de->ne', xf, w_q).reshape(N, PEER_HEADS, 2, PEER_KEY_DIM // 2)
    sc = jnp.einsum('nhpc,pkc->nhpk', q, sub_keys, preferred_element_type=jnp.float32)
    s_top, i_top = lax.top_k(sc, PEER_TOPK)
    cand = s_top[:, :, 0, :, None] + s_top[:, :, 1, None, :]
    c_top, c_idx = lax.top_k(cand.reshape(N, PEER_HEADS, PEER_TOPK * PEER_TOPK), PEER_TOPK)
    e1 = jnp.take_along_axis(i_top[:, :, 0], c_idx // PEER_TOPK, axis=-1)
    e2 = jnp.take_along_axis(i_top[:, :, 1], c_idx % PEER_TOPK, axis=-1)
    experts = (e1 * N_SUBKEYS + e2).reshape(N, PEER_HEADS * PEER_TOPK)
    gates = jax.nn.softmax(c_top, axis=-1).reshape(N, PEER_HEADS * PEER_TOPK)
    n_blk = -(-N // PEER_TOKEN_BLOCK)
    pad = n_blk * PEER_TOKEN_BLOCK - N
    xb = jnp.pad(xf, ((0, pad), (0, 0))).reshape(n_blk, PEER_TOKEN_BLOCK, D)
    eb = jnp.pad(experts, ((0, pad), (0, 0))).reshape(n_blk, PEER_TOKEN_BLOCK, -1)
    gb = jnp.pad(gates, ((0, pad), (0, 0))).reshape(n_blk, PEER_TOKEN_BLOCK, -1)

    def block(args):
        xt, et, gt = args
        u = u_tab[et]
        h = jnp.einsum('nd,nkd->nk', xt, u, preferred_element_type=jnp.float32)
        a = (gt * jax.nn.gelu(h, approximate=False)).astype(v_tab.dtype)
        return jnp.einsum('nk,nkd->nd', a, v_tab[et])

    y = lax.map(block, (xb, eb, gb))
    return y.reshape(-1, D)[:N].reshape(B, T, D).astype(x.dtype)


def layer_tail(h, att, conv, p_i, w_out, ln1_g, ln1_b, peer_wq, peer_subkeys, peer_u, peer_v,
               ln2_g, ln2_b, ple_gate, ple_proj):
    B, T, _ = h.shape
    mixed = jnp.concatenate([att.reshape(B, T, D_ATT).astype(h.dtype), conv], axis=-1)
    mixed = jnp.einsum('bte,ed->btd', mixed, w_out)
    h = layer_norm(DEEPNORM_ALPHA * h + mixed, ln1_g, ln1_b)
    h = layer_norm(DEEPNORM_ALPHA * h + peer_ffn(h, peer_wq, peer_subkeys, peer_u, peer_v), ln2_g, ln2_b)
    gate = jax.nn.sigmoid(jnp.einsum('btd,de->bte', h, ple_gate))
    return h + gate * jnp.einsum('btp,pd->btd', p_i, ple_proj)


def setup_inputs(seed: int = 0) -> dict:
    key = jax.random.key(seed)
    ks = jax.random.split(key, 24)
    nrm = lambda k, shape, s: jax.random.normal(k, shape, jnp.float32) * s
    n_buf = min(ATT_WINDOW, PAST_LEN)
    d_in = 3 * D_ATT + 3 * D_CONV
    return {
        "x_prompt": nrm(ks[0], (BATCH, SEQ, D_MODEL), 1.0),
        "x_sample": nrm(ks[1], (DEC_BATCH, DEC_SEQ, D_MODEL), 1.0),
        "state_attn_k": nrm(ks[2], (DEPTH, DEC_BATCH, n_buf, ATT_HEADS, HEAD_DIM), 1.0),
        "state_attn_v": nrm(ks[3], (DEPTH, DEC_BATCH, n_buf, ATT_HEADS, HEAD_DIM), 1.0),
        "state_conv": nrm(ks[4], (DEPTH, DEC_BATCH, CONV_WIDTH - 1, D_CONV), 1.0),
        "p_prompt": nrm(ks[5], (DEPTH, BATCH, SEQ, PLE_DIM), 1.0),
        "p_sample": nrm(ks[6], (DEPTH, DEC_BATCH, DEC_SEQ, PLE_DIM), 1.0),
        "w_in": nrm(ks[7], (DEPTH, D_MODEL, d_in), D_MODEL ** -0.5),
        "w_conv": nrm(ks[8], (DEPTH, CONV_WIDTH, D_CONV), CONV_WIDTH ** -0.5),
        "w_out": nrm(ks[9], (DEPTH, D_MODEL, D_MODEL), D_MODEL ** -0.5 * DEEPNORM_BETA),
        "ln1_g": 1.0 + nrm(ks[10], (DEPTH, D_MODEL), 0.01),
        "ln1_b": nrm(ks[11], (DEPTH, D_MODEL), 0.01),
        "peer_wq": nrm(ks[12], (DEPTH, D_MODEL, PEER_HEADS * PEER_KEY_DIM), D_MODEL ** -0.5),
        "peer_subkeys": nrm(ks[13], (DEPTH, 2, N_SUBKEYS, PEER_KEY_DIM // 2), (PEER_KEY_DIM // 2) ** -0.5),
        "peer_u": nrm(ks[14], (DEPTH, N_EXPERTS, D_MODEL), D_MODEL ** -0.5),
        "peer_v": nrm(ks[15], (DEPTH, N_EXPERTS, D_MODEL), DEEPNORM_BETA),
        "ln2_g": 1.0 + nrm(ks[16], (DEPTH, D_MODEL), 0.01),
        "ln2_b": nrm(ks[17], (DEPTH, D_MODEL), 0.01),
        "ple_gate": nrm(ks[18], (DEPTH, D_MODEL, D_MODEL), D_MODEL ** -0.5),
        "ple_proj": nrm(ks[19], (DEPTH, PLE_DIM, D_MODEL), PLE_DIM ** -0.5),
    }


def reference(x_prompt, x_sample, state_attn_k, state_attn_v, state_conv, p_prompt, p_sample,
              w_in, w_conv, w_out, ln1_g, ln1_b, peer_wq, peer_subkeys, peer_u, peer_v,
              ln2_g, ln2_b, ple_gate, ple_proj):
    t_p = x_prompt.shape[1]
    t_s = x_sample.shape[1]
    pos_p = jnp.arange(t_p)
    pos_s = PAST_LEN + jnp.arange(t_s)
    h_p, h_s = x_prompt, x_sample
    kp, vp, cp, ksl, vsl, csl = [], [], [], [], [], []
    for i in range(DEPTH):
        tail = (w_out[i], ln1_g[i], ln1_b[i], peer_wq[i], peer_subkeys[i], peer_u[i], peer_v[i],
                ln2_g[i], ln2_b[i], ple_gate[i], ple_proj[i])
        q, k, v, z, gb = split_projection(h_p, w_in[i])
        q, k = rope(q, pos_p), rope(k, pos_p)
        att = combine_branches([dilated_branch_prompt(q, k, v, w // d, d) for w, d in DIL_PAIRS])
        z_ext = jnp.pad(z, ((0, 0), (CONV_WIDTH - 1, 0), (0, 0)))
        conv = gb * short_conv(z_ext, w_conv[i], t_p)
        keep_p = min(ATT_WINDOW, t_p)
        kp.append(k[:, t_p - keep_p:])
        vp.append(v[:, t_p - keep_p:])
        cp.append(z_ext[:, -(CONV_WIDTH - 1):])
        h_p = layer_tail(h_p, att, conv, p_prompt[i], *tail)
        q, k, v, z, gb = split_projection(h_s, w_in[i])
        q, k = rope(q, pos_s), rope(k, pos_s)
        k_all = jnp.concatenate([state_attn_k[i].astype(k.dtype), k], axis=1)
        v_all = jnp.concatenate([state_attn_v[i].astype(v.dtype), v], axis=1)
        att = combine_branches([dilated_branch_sample(q, k_all, v_all, w // d, d) for w, d in DIL_PAIRS])
        z_ext = jnp.concatenate([state_conv[i].astype(z.dtype), z], axis=1)
        conv = gb * short_conv(z_ext, w_conv[i], t_s)
        n_all = k_all.shape[1]
        keep_s = min(ATT_WINDOW, n_all)
        ksl.append(k_all[:, n_all - keep_s:])
        vsl.append(v_all[:, n_all - keep_s:])
        csl.append(z_ext[:, -(CONV_WIDTH - 1):])
        h_s = layer_tail(h_s, att, conv, p_sample[i], *tail)
    y_prompt, y_sample = h_p, h_s
    new_k_prompt, new_v_prompt, new_conv_prompt = jnp.stack(kp), jnp.stack(vp), jnp.stack(cp)
    new_k_sample, new_v_sample, new_conv_sample = jnp.stack(ksl), jnp.stack(vsl), jnp.stack(csl)
    return (y_prompt, y_sample, new_k_prompt, new_v_prompt, new_conv_prompt,
            new_k_sample, new_v_sample, new_conv_sample)
```

```python
import functools
import math

import jax
import jax.numpy as jnp
from jax import lax
from jax.experimental import pallas as pl
from jax.experimental.pallas import tpu as pltpu

F32 = jnp.float32
BF16 = jnp.bfloat16
I32 = jnp.int32

D_MODEL = 2048
PAST_LEN = 16384
ATT_HEADS = 8
HEAD_DIM = 128
D_ATT = ATT_HEADS * HEAD_DIM
D_CONV = D_MODEL - D_ATT
CONV_WIDTH = 3
DILATIONS = (1, 4, 16)
BAND = 128
ROPE_THETA = 10000.0
PEER_HEADS = 8
N_SUBKEYS = 128
PEER_KEY_DIM = 256
PEER_TOPK = 16
PEER_SLOTS = PEER_HEADS * PEER_TOPK
DEPTH = 1
DEEPNORM_ALPHA = (2.0 * DEPTH) ** 0.25
LN_EPS = 1e-5

LANES = 128
SUBLANES = 8
ROW_TILE = 256
VMEM_LIMIT = 56 * 1024 * 1024
NEG_INF = float("-inf")


def _params(*sem):
    return pltpu.CompilerParams(dimension_semantics=sem, vmem_limit_bytes=VMEM_LIMIT)


def _resident(shape):
    nd = len(shape)
    return pl.BlockSpec(shape, lambda *_: (0,) * nd, pipeline_mode=pl.Buffered(1))


def _layer_norm(y, g, b):
    mu = jnp.mean(y, axis=-1, keepdims=True)
    c = y - mu
    var = jnp.mean(c * c, axis=-1, keepdims=True)
    return c * lax.rsqrt(var + LN_EPS) * g + b


def _in_proj_kernel(x_ref, w_ref, cos_ref, sin_ref, q_ref, k_ref, v_ref, z_ref, gb_ref):
    xb = x_ref[...].astype(BF16)
    cos = cos_ref[...]
    sin = sin_ref[...]

    def proj(c):
        return jnp.dot(xb, w_ref[:, c * D_ATT:(c + 1) * D_ATT], preferred_element_type=F32)

    def store_rotary(a, o_ref):
        for h in range(ATT_HEADS):
            ah = a[:, h * HEAD_DIM:(h + 1) * HEAD_DIM]
            o_ref[:, h * HEAD_DIM:(h + 1) * HEAD_DIM] = (
                ah * cos + pltpu.roll(ah, HEAD_DIM // 2, axis=1) * sin)

    store_rotary(proj(0), q_ref)
    store_rotary(proj(1), k_ref)
    v_ref[...] = proj(2)
    u = proj(3)
    gb_ref[...] = proj(4)
    z_ref[...] = proj(5) * u


def _in_proj(x, w_bf16, cos_tab, sin_tab):
    n = x.shape[0]
    tm = min(ROW_TILE, n)
    n_tab = cos_tab.shape[0] // tm
    row = lambda i: (i, 0)
    out = jax.ShapeDtypeStruct((n, D_ATT), F32)
    return pl.pallas_call(
        _in_proj_kernel,
        grid=(n // tm,),
        in_specs=[pl.BlockSpec((tm, D_MODEL), row),
                  _resident(w_bf16.shape),
                  pl.BlockSpec((tm, HEAD_DIM), lambda i: (i % n_tab, 0)),
                  pl.BlockSpec((tm, HEAD_DIM), lambda i: (i % n_tab, 0))],
        out_specs=[pl.BlockSpec((tm, D_ATT), row)] * 5,
        out_shape=[out] * 5,
        compiler_params=_params("parallel"),
        name="in_proj",
    )(x, w_bf16, cos_tab, sin_tab)


def _conv_kernel(z_ref, zprev_ref, st_ref, gb_ref, wc_ref, o_ref):
    halo = jnp.where(pl.program_id(1) == 0, st_ref[...], zprev_ref[...])
    z = z_ref[...]
    row = lax.broadcasted_iota(I32, z.shape, 0)
    h1 = halo[SUBLANES - 1:SUBLANES, :]
    h2 = halo[SUBLANES - 2:SUBLANES - 1, :]
    zm1 = jnp.where(row == 0, h1, pltpu.roll(z, 1, axis=0))
    zm2 = jnp.where(row == 0, h2, jnp.where(row == 1, h1, pltpu.roll(z, 2, axis=0)))
    wc = wc_ref[...]
    o_ref[...] = gb_ref[...] * (wc[0:1, :] * zm2 + wc[1:2, :] * zm1 + wc[2:3, :] * z)


def _conv(z, gb, state_halo, w_conv, batch, seq):
    tt = min(ROW_TILE, seq)
    per_seq = seq // tt
    halo_per_tile = tt // SUBLANES
    z3 = z.reshape(batch, seq, D_CONV)
    gb3 = gb.reshape(batch, seq, D_CONV)
    tile = pl.BlockSpec((None, tt, D_CONV), lambda b, i: (b, i, 0))
    out = pl.pallas_call(
        _conv_kernel,
        grid=(batch, per_seq),
        in_specs=[tile,
                  pl.BlockSpec((None, SUBLANES, D_CONV),
                               lambda b, i: (b, jnp.maximum(i * halo_per_tile - 1, 0), 0)),
                  pl.BlockSpec((None, SUBLANES, D_CONV), lambda b, i: (b, 0, 0)),
                  tile,
                  pl.BlockSpec((CONV_WIDTH, D_CONV), lambda b, i: (0, 0))],
        out_specs=tile,
        out_shape=jax.ShapeDtypeStruct((batch, seq, D_CONV), F32),
        compiler_params=_params("parallel", "parallel"),
        name="conv",
    )(z3, z3, state_halo, gb3, w_conv)
    return out.reshape(batch * seq, D_CONV)


def _dot_t(a, b):
    return lax.dot_general(a, b, (((1,), (1,)), ((), ())), preferred_element_type=F32)


def _attn_prompt_kernel(q_ref, k_ref, v_ref, o_ref, acc_sc, m_sc, l_sc):
    seq = q_ref.shape[0]
    scale = HEAD_DIM ** -0.5
    qi = lax.broadcasted_iota(I32, (BAND, BAND), 0)
    kj = lax.broadcasted_iota(I32, (BAND, BAND), 1)
    cur_ok = kj <= qi
    prev_ok = kj >= qi

    for bi, d in enumerate(DILATIONS):
        nb = seq // (d * BAND)

        def block(blk, carry, bi=bi, d=d, nb=nb):
            r = blk // nb
            n = blk % nb
            start = r + n * (d * BAND)
            prev_start = jnp.maximum(start - d * BAND, 0)
            rows = pl.ds(start, BAND, stride=d) if d > 1 else pl.ds(pl.multiple_of(start, BAND), BAND)
            prows = pl.ds(prev_start, BAND, stride=d) if d > 1 else pl.ds(pl.multiple_of(prev_start, BAND), BAND)
            q = q_ref[rows, :].astype(BF16)
            s_cur = _dot_t(q, k_ref[rows, :].astype(BF16)) * scale
            s_prev = _dot_t(q, k_ref[prows, :].astype(BF16)) * scale
            s_cur = jnp.where(cur_ok, s_cur, NEG_INF)
            s_prev = jnp.where(prev_ok, s_prev, NEG_INF) + jnp.where(n > 0, 0.0, NEG_INF)
            m = jnp.maximum(jnp.max(s_cur, axis=1, keepdims=True),
                            jnp.max(s_prev, axis=1, keepdims=True))
            p_cur = jnp.exp(s_cur - m)
            p_prev = jnp.exp(s_prev - m)
            l = jnp.sum(p_cur, axis=1, keepdims=True) + jnp.sum(p_prev, axis=1, keepdims=True)
            acc = (jnp.dot(p_cur.astype(BF16), v_ref[rows, :].astype(BF16), preferred_element_type=F32)
                   + jnp.dot(p_prev.astype(BF16), v_ref[prows, :].astype(BF16), preferred_element_type=F32))
            acc_sc[bi, rows, :] = acc
            m_sc[bi, rows, :] = jnp.broadcast_to(m, (BAND, HEAD_DIM))
            l_sc[bi, rows, :] = jnp.broadcast_to(l, (BAND, HEAD_DIM))
            return carry

        lax.fori_loop(0, d * nb, block, 0)

    big_m = jnp.maximum(jnp.maximum(m_sc[0], m_sc[1]), m_sc[2])
    num = jnp.zeros((seq, HEAD_DIM), F32)
    den = jnp.zeros((seq, HEAD_DIM), F32)
    for bi in range(len(DILATIONS)):
        w = jnp.exp(m_sc[bi] - big_m)
        num = num + w * acc_sc[bi]
        den = den + w * l_sc[bi]
    o_ref[...] = num / den


def _attn_prompt(q, k, v, batch, seq):
    q3, k3, v3 = (a.reshape(batch, seq, D_ATT) for a in (q, k, v))
    head = pl.BlockSpec((None, seq, HEAD_DIM), lambda b, h: (b, 0, h))
    nbr = len(DILATIONS)
    out = pl.pallas_call(
        _attn_prompt_kernel,
        grid=(batch, ATT_HEADS),
        in_specs=[head, head, head],
        out_specs=head,
        out_shape=jax.ShapeDtypeStruct((batch, seq, D_ATT), F32),
        scratch_shapes=[pltpu.VMEM((nbr, seq, HEAD_DIM), F32)] * 3,
        compiler_params=_params("parallel", "parallel"),
        name="attn_prompt",
    )(q3, k3, v3)
    return out.reshape(batch * seq, D_ATT)


def _branch_count(rel):
    cnt = jnp.zeros(rel.shape, F32)
    for d in DILATIONS:
        hit = (rel >= 0) & (rel <= d * BAND) & ((rel & (d - 1)) == 0)
        cnt = cnt + jnp.where(hit, 1.0, 0.0)
    return cnt


def _attn_sample_kernel(q_ref, kn_ref, vn_ref, ks_ref, vs_ref, o_ref, ko_ref, vo_ref):
    n_buf = ks_ref.shape[0]
    t_new = q_ref.shape[0]
    scale = HEAD_DIM ** -0.5
    q = q_ref[...].astype(BF16)
    ks = ks_ref[...]
    vs = vs_ref[...]
    kn = kn_ref[...]
    vn = vn_ref[...]
    s_st = _dot_t(q, ks.astype(BF16)) * scale
    s_nw = _dot_t(q, kn.astype(BF16)) * scale
    qpos = n_buf + lax.broadcasted_iota(I32, s_st.shape, 0)
    c_st = _branch_count(qpos - lax.broadcasted_iota(I32, s_st.shape, 1))
    c_nw = _branch_count(lax.broadcasted_iota(I32, s_nw.shape, 0)
                         - lax.broadcasted_iota(I32, s_nw.shape, 1))
    s_st = jnp.where(c_st > 0, s_st, NEG_INF)
    s_nw = jnp.where(c_nw > 0, s_nw, NEG_INF)
    m = jnp.maximum(jnp.max(s_st, axis=1, keepdims=True), jnp.max(s_nw, axis=1, keepdims=True))
    p_st = c_st * jnp.exp(s_st - m)
    p_nw = c_nw * jnp.exp(s_nw - m)
    l = jnp.sum(p_st, axis=1, keepdims=True) + jnp.sum(p_nw, axis=1, keepdims=True)
    acc = (jnp.dot(p_st.astype(BF16), vs.astype(BF16), preferred_element_type=F32)
           + jnp.dot(p_nw.astype(BF16), vn.astype(BF16), preferred_element_type=F32))
    o_ref[...] = acc / l
    keep = n_buf - t_new
    ko_ref[0:keep, :] = ks[t_new:, :]
    ko_ref[keep:, :] = kn
    vo_ref[0:keep, :] = vs[t_new:, :]
    vo_ref[keep:, :] = vn


def _attn_sample(q, k_new, v_new, k_state, v_state, batch, t_new):
    n_buf = k_state.shape[1]
    new = pl.BlockSpec((t_new, HEAD_DIM), lambda b, h: (b, h))
    buf = pl.BlockSpec((None, n_buf, HEAD_DIM), lambda b, h: (b, 0, h))
    buf_shape = jax.ShapeDtypeStruct((batch, n_buf, D_ATT), F32)
    return pl.pallas_call(
        _attn_sample_kernel,
        grid=(batch, ATT_HEADS),
        in_specs=[new, new, new, buf, buf],
        out_specs=[new, buf, buf],
        out_shape=[jax.ShapeDtypeStruct((batch * t_new, D_ATT), F32), buf_shape, buf_shape],
        compiler_params=_params("parallel", "parallel"),
        name="attn_sample",
    )(q, k_new, v_new, k_state, v_state)


def _out_proj_kernel(att_ref, conv_ref, x_ref, w_ref, g_ref, b_ref, o_ref):
    mixed = (jnp.dot(att_ref[...].astype(BF16), w_ref[0:D_ATT, :], preferred_element_type=F32)
             + jnp.dot(conv_ref[...].astype(BF16), w_ref[D_ATT:D_MODEL, :], preferred_element_type=F32))
    o_ref[...] = _layer_norm(DEEPNORM_ALPHA * x_ref[...] + mixed, g_ref[...], b_ref[...])


def _out_proj(att, conv, x, w_bf16, g, b):
    n = x.shape[0]
    tm = min(ROW_TILE, n)
    row = lambda i: (i, 0)
    return pl.pallas_call(
        _out_proj_kernel,
        grid=(n // tm,),
        in_specs=[pl.BlockSpec((tm, D_ATT), row), pl.BlockSpec((tm, D_CONV), row),
                  pl.BlockSpec((tm, D_MODEL), row), _resident(w_bf16.shape),
                  _resident(g.shape), _resident(b.shape)],
        out_specs=pl.BlockSpec((tm, D_MODEL), row),
        out_shape=jax.ShapeDtypeStruct((n, D_MODEL), F32),
        compiler_params=_params("parallel"),
        name="out_proj",
    )(att, conv, x, w_bf16, g, b)


def _top_rows(s, k):
    rows = s.shape[0]
    iota = lax.broadcasted_iota(I32, s.shape, 0)
    vals, idxs = [], []
    for _ in range(k):
        m = jnp.max(s, axis=0, keepdims=True)
        idx = jnp.min(jnp.where(s == m, iota, rows), axis=0, keepdims=True)
        vals.append(m)
        idxs.append(idx)
        s = jnp.where(iota == idx, NEG_INF, s)
    return jnp.concatenate(vals, axis=0), jnp.concatenate(idxs, axis=0)


def _take_rows(table, idx):
    out = jnp.zeros(idx.shape, table.dtype)
    for i in range(table.shape[0]):
        out = jnp.where(idx == i, table[i:i + 1, :], out)
    return out


def _peer_route_kernel(h_ref, wq_ref, sk_ref, e_ref, g_ref):
    hb = h_ref[...].astype(BF16)
    half = PEER_KEY_DIM // 2
    experts, gates = [], []
    for head in range(PEER_HEADS):
        top = []
        for p in range(2):
            c0 = head * PEER_KEY_DIM + p * half
            qhp = jnp.dot(hb, wq_ref[:, c0:c0 + half], preferred_element_type=F32)
            sc = _dot_t(sk_ref[p], qhp.astype(BF16))
            top.append(_top_rows(sc, PEER_TOPK))
        (v1, i1), (v2, i2) = top
        cand = jnp.concatenate([v1[i:i + 1, :] + v2 for i in range(PEER_TOPK)], axis=0)
        c_top, c_idx = _top_rows(cand, PEER_TOPK)
        e1 = _take_rows(i1, lax.shift_right_logical(c_idx, int(math.log2(PEER_TOPK))))
        e2 = _take_rows(i2, c_idx & (PEER_TOPK - 1))
        experts.append(e1 * N_SUBKEYS + e2)
        ex = jnp.exp(c_top - jnp.max(c_top, axis=0, keepdims=True))
        gates.append(ex / jnp.sum(ex, axis=0, keepdims=True))
    e_ref[...] = jnp.concatenate(experts, axis=0).T
    g_ref[...] = jnp.concatenate(gates, axis=0).T


def _peer_route(h, wq_bf16, subkeys_bf16):
    n = h.shape[0]
    tm = min(ROW_TILE, n)
    row = lambda i: (i, 0)
    return pl.pallas_call(
        _peer_route_kernel,
        grid=(n // tm,),
        in_specs=[pl.BlockSpec((tm, D_MODEL), row), _resident(wq_bf16.shape),
                  _resident(subkeys_bf16.shape)],
        out_specs=[pl.BlockSpec((tm, PEER_SLOTS), row)] * 2,
        out_shape=[jax.ShapeDtypeStruct((n, PEER_SLOTS), I32),
                   jax.ShapeDtypeStruct((n, PEER_SLOTS), F32)],
        compiler_params=_params("parallel"),
        name="peer_route",
    )(h, wq_bf16, subkeys_bf16)


PEER_TOKENS = 64


def _peer_expert_kernel(e_ref, h_ref, g_ref, u_hbm, v_hbm, o_ref, ubuf, vbuf, sem):
    tokens = h_ref.shape[0]
    eye = (lax.broadcasted_iota(I32, (PEER_SLOTS, PEER_SLOTS), 0)
           == lax.broadcasted_iota(I32, (PEER_SLOTS, PEER_SLOTS), 1))

    def start(t, slot):
        for k in range(PEER_SLOTS):
            e = e_ref[t, k]
            pltpu.make_async_copy(u_hbm.at[pl.ds(e, 1)], ubuf.at[slot, pl.ds(k, 1)], sem.at[0, slot]).start()
            pltpu.make_async_copy(v_hbm.at[pl.ds(e, 1)], vbuf.at[slot, pl.ds(k, 1)], sem.at[1, slot]).start()

    def wait(slot):
        pltpu.make_async_copy(u_hbm.at[pl.ds(0, PEER_SLOTS)], ubuf.at[slot], sem.at[0, slot]).wait()
        pltpu.make_async_copy(v_hbm.at[pl.ds(0, PEER_SLOTS)], vbuf.at[slot], sem.at[1, slot]).wait()

    start(0, 0)

    def token(t, carry):
        slot = t % 2

        @pl.when(t + 1 < tokens)
        def _():
            start(t + 1, 1 - slot)

        wait(slot)
        x = h_ref[pl.ds(t, 1), :]
        hid = jnp.sum(ubuf[slot] * x, axis=1, keepdims=True)
        gate = jnp.sum(jnp.where(eye, g_ref[pl.ds(t, 1), :], 0.0), axis=1, keepdims=True)
        a = gate * (0.5 * hid * (1.0 + lax.erf(hid * (2.0 ** -0.5))))
        o_ref[pl.ds(t, 1), :] = jnp.sum(a * vbuf[slot], axis=0, keepdims=True)
        return carry

    lax.fori_loop(0, tokens, token, 0)


def _peer_expert(experts, h, gates, u_tab, v_tab):
    n = h.shape[0]
    tb = min(PEER_TOKENS, n)
    row = lambda i: (i, 0)
    buf = pltpu.VMEM((2, PEER_SLOTS, D_MODEL), F32)
    return pl.pallas_call(
        _peer_expert_kernel,
        grid=(n // tb,),
        in_specs=[pl.BlockSpec((tb, PEER_SLOTS), row, memory_space=pltpu.SMEM),
                  pl.BlockSpec((tb, D_MODEL), row),
                  pl.BlockSpec((tb, PEER_SLOTS), row),
                  pl.BlockSpec(memory_space=pl.ANY),
                  pl.BlockSpec(memory_space=pl.ANY)],
        out_specs=pl.BlockSpec((tb, D_MODEL), row),
        out_shape=jax.ShapeDtypeStruct((n, D_MODEL), F32),
        scratch_shapes=[buf, buf, pltpu.SemaphoreType.DMA((2, 2))],
        compiler_params=_params("arbitrary"),
        name="peer_expert",
    )(experts, h, gates, u_tab, v_tab)


def _tail_kernel(h_ref, y_ref, p_ref, g_ref, b_ref, wg_ref, wp_ref, o_ref):
    h2 = _layer_norm(DEEPNORM_ALPHA * h_ref[...] + y_ref[...], g_ref[...], b_ref[...])
    gate = jax.nn.sigmoid(jnp.dot(h2.astype(BF16), wg_ref[...], preferred_element_type=F32))
    emb = jnp.dot(p_ref[...].astype(BF16), wp_ref[...], preferred_element_type=F32)
    o_ref[...] = h2 + gate * emb


def _tail(h, y, p, g, b, wg_bf16, wp_bf16):
    n = h.shape[0]
    tm = min(ROW_TILE, n)
    row = lambda i: (i, 0)
    return pl.pallas_call(
        _tail_kernel,
        grid=(n // tm,),
        in_specs=[pl.BlockSpec((tm, D_MODEL), row), pl.BlockSpec((tm, D_MODEL), row),
                  pl.BlockSpec((tm, p.shape[1]), row), _resident(g.shape), _resident(b.shape),
                  _resident(wg_bf16.shape), _resident(wp_bf16.shape)],
        out_specs=pl.BlockSpec((tm, D_MODEL), row),
        out_shape=jax.ShapeDtypeStruct((n, D_MODEL), F32),
        compiler_params=_params("parallel"),
        name="tail",
    )(h, y, p, g, b, wg_bf16, wp_bf16)


def _rotary_tables(pos):
    half = HEAD_DIM // 2
    inv = ROPE_THETA ** (-jnp.arange(half, dtype=F32) / half)
    ang = pos.astype(F32)[:, None] * inv[None, :]
    cos, sin = jnp.cos(ang), jnp.sin(ang)
    return jnp.concatenate([cos, cos], axis=1), jnp.concatenate([-sin, sin], axis=1)


def _state_halo(state):
    return jnp.pad(state, ((0, 0), (SUBLANES - (CONV_WIDTH - 1), 0), (0, 0)))


def _layer_tail(x, att, conv, p, w):
    h1 = _out_proj(att, conv, x, w["w_out"], w["ln1_g"], w["ln1_b"])
    experts, gates = _peer_route(h1, w["peer_wq"], w["peer_subkeys"])
    y = _peer_expert(experts, h1, gates, w["peer_u"], w["peer_v"])
    return _tail(h1, y, p, w["ln2_g"], w["ln2_b"], w["ple_gate"], w["ple_proj"])


def kernel(x_prompt, x_sample, state_attn_k, state_attn_v, state_conv, p_prompt, p_sample,
           w_in, w_conv, w_out, ln1_g, ln1_b, peer_wq, peer_subkeys, peer_u, peer_v,
           ln2_g, ln2_b, ple_gate, ple_proj):
    assert w_in.shape[0] == DEPTH
    bp, tp, _ = x_prompt.shape
    bs, ts, _ = x_sample.shape
    n_buf = state_attn_k.shape[2]
    w = {
        "w_in": w_in[0].astype(BF16), "w_out": w_out[0].astype(BF16),
        "ln1_g": ln1_g, "ln1_b": ln1_b, "ln2_g": ln2_g, "ln2_b": ln2_b,
        "peer_wq": peer_wq[0].astype(BF16), "peer_subkeys": peer_subkeys[0].astype(BF16),
        "peer_u": peer_u[0], "peer_v": peer_v[0],
        "ple_gate": ple_gate[0].astype(BF16), "ple_proj": ple_proj[0].astype(BF16),
    }
    wc = w_conv[0]

    xp = x_prompt.reshape(bp * tp, D_MODEL)
    cos_p, sin_p = _rotary_tables(jnp.arange(tp))
    q, k, v, z, gb = _in_proj(xp, w["w_in"], cos_p, sin_p)
    conv = _conv(z, gb, jnp.zeros((bp, SUBLANES, D_CONV), F32), wc, bp, tp)
    att = _attn_prompt(q, k, v, bp, tp)
    y_prompt = _layer_tail(xp, att, conv, p_prompt[0].reshape(bp * tp, -1), w).reshape(bp, tp, D_MODEL)
    new_k_prompt = k.reshape(1, bp, tp, ATT_HEADS, HEAD_DIM)
    new_v_prompt = v.reshape(1, bp, tp, ATT_HEADS, HEAD_DIM)
    new_conv_prompt = z.reshape(bp, tp, D_CONV)[:, tp - (CONV_WIDTH - 1):][None]

    xs = x_sample.reshape(bs * ts, D_MODEL)
    cos_s, sin_s = _rotary_tables(jnp.tile(PAST_LEN + jnp.arange(ts), bs))
    q, k, v, z, gb = _in_proj(xs, w["w_in"], cos_s, sin_s)
    conv = _conv(z, gb, _state_halo(state_conv[0]), wc, bs, ts)
    att, new_k, new_v = _attn_sample(q, k, v, state_attn_k[0].reshape(bs, n_buf, D_ATT),
                                     state_attn_v[0].reshape(bs, n_buf, D_ATT), bs, ts)
    y_sample = _layer_tail(xs, att, conv, p_sample[0].reshape(bs * ts, -1), w).reshape(bs, ts, D_MODEL)
    new_k_sample = new_k.reshape(1, bs, n_buf, ATT_HEADS, HEAD_DIM)
    new_v_sample = new_v.reshape(1, bs, n_buf, ATT_HEADS, HEAD_DIM)
    z_ext = jnp.concatenate([state_conv[0], z.reshape(bs, ts, D_CONV)], axis=1)
    new_conv_sample = z_ext[:, -(CONV_WIDTH - 1):][None]

    return (y_prompt, y_sample, new_k_prompt, new_v_prompt, new_conv_prompt,
            new_k_sample, new_v_sample, new_conv_sample)
```

```python
import functools
import math

import jax
import jax.numpy as jnp
from jax import lax
from jax.experimental import pallas as pl
from jax.experimental.pallas import tpu as pltpu

F32 = jnp.float32
BF16 = jnp.bfloat16
I32 = jnp.int32

D_MODEL = 2048
PAST_LEN = 16384
ATT_HEADS = 8
HEAD_DIM = 128
D_ATT = ATT_HEADS * HEAD_DIM
D_CONV = D_MODEL - D_ATT
CONV_WIDTH = 3
DILATIONS = (1, 4, 16)
BAND = 128
ROPE_THETA = 10000.0
PEER_HEADS = 8
N_SUBKEYS = 128
PEER_KEY_DIM = 256
PEER_TOPK = 16
PEER_SLOTS = PEER_HEADS * PEER_TOPK
DEPTH = 1
DEEPNORM_ALPHA = (2.0 * DEPTH) ** 0.25
LN_EPS = 1e-5

LANES = 128
SUBLANES = 8
ROW_TILE = 256
VMEM_LIMIT = 56 * 1024 * 1024
NEG_INF = float("-inf")


def _params(*sem):
    return pltpu.CompilerParams(dimension_semantics=sem, vmem_limit_bytes=VMEM_LIMIT)


def _resident(shape):
    nd = len(shape)
    return pl.BlockSpec(shape, lambda *_: (0,) * nd, pipeline_mode=pl.Buffered(1))


def _layer_norm(y, g, b):
    mu = jnp.mean(y, axis=-1, keepdims=True)
    c = y - mu
    var = jnp.mean(c * c, axis=-1, keepdims=True)
    return c * lax.rsqrt(var + LN_EPS) * g + b


def _in_proj_kernel(x_ref, w_ref, cos_ref, sin_ref, q_ref, k_ref, v_ref, z_ref, gb_ref):
    xb = x_ref[...].astype(BF16)
    cos = cos_ref[...]
    sin = sin_ref[...]

    def proj(c):
        return jnp.dot(xb, w_ref[:, c * D_ATT:(c + 1) * D_ATT], preferred_element_type=F32)

    def store_rotary(a, o_ref):
        for h in range(ATT_HEADS):
            ah = a[:, h * HEAD_DIM:(h + 1) * HEAD_DIM]
            o_ref[:, h * HEAD_DIM:(h + 1) * HEAD_DIM] = (
                ah * cos + pltpu.roll(ah, HEAD_DIM // 2, axis=1) * sin)

    store_rotary(proj(0), q_ref)
    store_rotary(proj(1), k_ref)
    v_ref[...] = proj(2)
    u = proj(3)
    gb_ref[...] = proj(4)
    z_ref[...] = proj(5) * u


def _in_proj(x, w_bf16, cos_tab, sin_tab):
    n = x.shape[0]
    tm = min(ROW_TILE, n)
    n_tab = cos_tab.shape[0] // tm
    row = lambda i: (i, 0)
    out = jax.ShapeDtypeStruct((n, D_ATT), F32)
    return pl.pallas_call(
        _in_proj_kernel,
        grid=(n // tm,),
        in_specs=[pl.BlockSpec((tm, D_MODEL), row),
                  _resident(w_bf16.shape),
                  pl.BlockSpec((tm, HEAD_DIM), lambda i: (i % n_tab, 0)),
                  pl.BlockSpec((tm, HEAD_DIM), lambda i: (i % n_tab, 0))],
        out_specs=[pl.BlockSpec((tm, D_ATT), row)] * 5,
        out_shape=[out] * 5,
        compiler_params=_params("parallel"),
        name="in_proj",
    )(x, w_bf16, cos_tab, sin_tab)


def _conv_kernel(z_ref, zprev_ref, st_ref, gb_ref, wc_ref, o_ref):
    halo = jnp.where(pl.program_id(1) == 0, st_ref[...], zprev_ref[...])
    z = z_ref[...]
    row = lax.broadcasted_iota(I32, z.shape, 0)
    h1 = halo[SUBLANES - 1:SUBLANES, :]
    h2 = halo[SUBLANES - 2:SUBLANES - 1, :]
    zm1 = jnp.where(row == 0, h1, pltpu.roll(z, 1, axis=0))
    zm2 = jnp.where(row == 0, h2, jnp.where(row == 1, h1, pltpu.roll(z, 2, axis=0)))
    wc = wc_ref[...]
    o_ref[...] = gb_ref[...] * (wc[0:1, :] * zm2 + wc[1:2, :] * zm1 + wc[2:3, :] * z)


def _conv(z, gb, state_halo, w_conv, batch, seq):
    tt = min(ROW_TILE, seq)
    per_seq = seq // tt
    halo_per_tile = tt // SUBLANES
    z3 = z.reshape(batch, seq, D_CONV)
    gb3 = gb.reshape(batch, seq, D_CONV)
    tile = pl.BlockSpec((None, tt, D_CONV), lambda b, i: (b, i, 0))
    out = pl.pallas_call(
        _conv_kernel,
        grid=(batch, per_seq),
        in_specs=[tile,
                  pl.BlockSpec((None, SUBLANES, D_CONV),
                               lambda b, i: (b, jnp.maximum(i * halo_per_tile - 1, 0), 0)),
                  pl.BlockSpec((None, SUBLANES, D_CONV), lambda b, i: (b, 0, 0)),
                  tile,
                  pl.BlockSpec((CONV_WIDTH, D_CONV), lambda b, i: (0, 0))],
        out_specs=tile,
        out_shape=jax.ShapeDtypeStruct((batch, seq, D_CONV), F32),
        compiler_params=_params("parallel", "parallel"),
        name="conv",
    )(z3, z3, state_halo, gb3, w_conv)
    return out.reshape(batch * seq, D_CONV)


def _dot_t(a, b):
    return lax.dot_general(a, b, (((1,), (1,)), ((), ())), preferred_element_type=F32)


def _attn_prompt_kernel(q_ref, k_ref, v_ref, o_ref, acc_sc, m_sc, l_sc):
    seq = q_ref.shape[0]
    scale = HEAD_DIM ** -0.5
    qi = lax.broadcasted_iota(I32, (BAND, BAND), 0)
    kj = lax.broadcasted_iota(I32, (BAND, BAND), 1)
    cur_ok = kj <= qi
    prev_ok = kj >= qi

    for bi, d in enumerate(DILATIONS):
        nb = seq // (d * BAND)

        def block(blk, carry, bi=bi, d=d, nb=nb):
            r = blk // nb
            n = blk % nb
            start = r + n * (d * BAND)
            prev_start = jnp.maximum(start - d * BAND, 0)
            rows = pl.ds(start, BAND, stride=d) if d > 1 else pl.ds(pl.multiple_of(start, BAND), BAND)
            prows = pl.ds(prev_start, BAND, stride=d) if d > 1 else pl.ds(pl.multiple_of(prev_start, BAND), BAND)
            q = q_ref[rows, :].astype(BF16)
            s_cur = _dot_t(q, k_ref[rows, :].astype(BF16)) * scale
            s_prev = _dot_t(q, k_ref[prows, :].astype(BF16)) * scale
            s_cur = jnp.where(cur_ok, s_cur, NEG_INF)
            s_prev = jnp.where(prev_ok, s_prev, NEG_INF) + jnp.where(n > 0, 0.0, NEG_INF)
            m = jnp.maximum(jnp.max(s_cur, axis=1, keepdims=True),
                            jnp.max(s_prev, axis=1, keepdims=True))
            p_cur = jnp.exp(s_cur - m)
            p_prev = jnp.exp(s_prev - m)
            l = jnp.sum(p_cur, axis=1, keepdims=True) + jnp.sum(p_prev, axis=1, keepdims=True)
            acc = (jnp.dot(p_cur.astype(BF16), v_ref[rows, :].astype(BF16), preferred_element_type=F32)
                   + jnp.dot(p_prev.astype(BF16), v_ref[prows, :].astype(BF16), preferred_element_type=F32))
            acc_sc[bi, rows, :] = acc
            m_sc[bi, rows, :] = jnp.broadcast_to(m, (BAND, HEAD_DIM))
            l_sc[bi, rows, :] = jnp.broadcast_to(l, (BAND, HEAD_DIM))
            return carry

        lax.fori_loop(0, d * nb, block, 0)

    big_m = jnp.maximum(jnp.maximum(m_sc[0], m_sc[1]), m_sc[2])
    num = jnp.zeros((seq, HEAD_DIM), F32)
    den = jnp.zeros((seq, HEAD_DIM), F32)
    for bi in range(len(DILATIONS)):
        w = jnp.exp(m_sc[bi] - big_m)
        num = num + w * acc_sc[bi]
        den = den + w * l_sc[bi]
    o_ref[...] = num / den


def _attn_prompt(q, k, v, batch, seq):
    q3, k3, v3 = (a.reshape(batch, seq, D_ATT) for a in (q, k, v))
    head = pl.BlockSpec((None, seq, HEAD_DIM), lambda b, h: (b, 0, h))
    nbr = len(DILATIONS)
    out = pl.pallas_call(
        _attn_prompt_kernel,
        grid=(batch, ATT_HEADS),
        in_specs=[head, head, head],
        out_specs=head,
        out_shape=jax.ShapeDtypeStruct((batch, seq, D_ATT), F32),
        scratch_shapes=[pltpu.VMEM((nbr, seq, HEAD_DIM), F32)] * 3,
        compiler_params=_params("parallel", "parallel"),
        name="attn_prompt",
    )(q3, k3, v3)
    return out.reshape(batch * seq, D_ATT)


def _branch_count(rel):
    cnt = jnp.zeros(rel.shape, F32)
    for d in DILATIONS:
        hit = (rel >= 0) & (rel <= d * BAND) & ((rel & (d - 1)) == 0)
        cnt = cnt + jnp.where(hit, 1.0, 0.0)
    return cnt


def _attn_sample_kernel(q_ref, kn_ref, vn_ref, ks_ref, vs_ref, o_ref, ko_ref, vo_ref):
    n_buf = ks_ref.shape[0]
    t_new = q_ref.shape[0]
    scale = HEAD_DIM ** -0.5
    q = q_ref[...].astype(BF16)
    ks = ks_ref[...]
    vs = vs_ref[...]
    kn = kn_ref[...]
    vn = vn_ref[...]
    s_st = _dot_t(q, ks.astype(BF16)) * scale
    s_nw = _dot_t(q, kn.astype(BF16)) * scale
    qpos = n_buf + lax.broadcasted_iota(I32, s_st.shape, 0)
    c_st = _branch_count(qpos - lax.broadcasted_iota(I32, s_st.shape, 1))
    c_nw = _branch_count(lax.broadcasted_iota(I32, s_nw.shape, 0)
                         - lax.broadcasted_iota(I32, s_nw.shape, 1))
    s_st = jnp.where(c_st > 0, s_st, NEG_INF)
    s_nw = jnp.where(c_nw > 0, s_nw, NEG_INF)
    m = jnp.maximum(jnp.max(s_st, axis=1, keepdims=True), jnp.max(s_nw, axis=1, keepdims=True))
    p_st = c_st * jnp.exp(s_st - m)
    p_nw = c_nw * jnp.exp(s_nw - m)
    l = jnp.sum(p_st, axis=1, keepdims=True) + jnp.sum(p_nw, axis=1, keepdims=True)
    acc = (jnp.dot(p_st.astype(BF16), vs.astype(BF16), preferred_element_type=F32)
           + jnp.dot(p_nw.astype(BF16), vn.astype(BF16), preferred_element_type=F32))
    o_ref[...] = acc / l
    keep = n_buf - t_new
    ko_ref[0:keep, :] = ks[t_new:, :]
    ko_ref[keep:, :] = kn
    vo_ref[0:keep, :] = vs[t_new:, :]
    vo_ref[keep:, :] = vn


def _attn_sample(q, k_new, v_new, k_state, v_state, batch, t_new):
    n_buf = k_state.shape[1]
    new = pl.BlockSpec((t_new, HEAD_DIM), lambda b, h: (b, h))
    buf = pl.BlockSpec((None, n_buf, HEAD_DIM), lambda b, h: (b, 0, h))
    buf_shape = jax.ShapeDtypeStruct((batch, n_buf, D_ATT), F32)
    return pl.pallas_call(
        _attn_sample_kernel,
        grid=(batch, ATT_HEADS),
        in_specs=[new, new, new, buf, buf],
        out_specs=[new, buf, buf],
        out_shape=[jax.ShapeDtypeStruct((batch * t_new, D_ATT), F32), buf_shape, buf_shape],
        compiler_params=_params("parallel", "parallel"),
        name="attn_sample",
    )(q, k_new, v_new, k_state, v_state)


def _out_proj_kernel(att_ref, conv_ref, x_ref, w_ref, g_ref, b_ref, o_ref):
    mixed = (jnp.dot(att_ref[...].astype(BF16), w_ref[0:D_ATT, :], preferred_element_type=F32)
             + jnp.dot(conv_ref[...].astype(BF16), w_ref[D_ATT:D_MODEL, :], preferred_element_type=F32))
    o_ref[...] = _layer_norm(DEEPNORM_ALPHA * x_ref[...] + mixed, g_ref[...], b_ref[...])


def _out_proj(att, conv, x, w_bf16, g, b):
    n = x.shape[0]
    tm = min(ROW_TILE, n)
    row = lambda i: (i, 0)
    return pl.pallas_call(
        _out_proj_kernel,
        grid=(n // tm,),
        in_specs=[pl.BlockSpec((tm, D_ATT), row), pl.BlockSpec((tm, D_CONV), row),
                  pl.BlockSpec((tm, D_MODEL), row), _resident(w_bf16.shape),
                  _resident(g.shape), _resident(b.shape)],
        out_specs=pl.BlockSpec((tm, D_MODEL), row),
        out_shape=jax.ShapeDtypeStruct((n, D_MODEL), F32),
        compiler_params=_params("parallel"),
        name="out_proj",
    )(att, conv, x, w_bf16, g, b)


def _top_rows(s, k):
    rows = s.shape[0]
    iota = lax.broadcasted_iota(I32, s.shape, 0)
    vals, idxs = [], []
    for _ in range(k):
        m = jnp.max(s, axis=0, keepdims=True)
        idx = jnp.min(jnp.where(s == m, iota, rows), axis=0, keepdims=True)
        vals.append(m)
        idxs.append(idx)
        s = jnp.where(iota == idx, NEG_INF, s)
    return jnp.concatenate(vals, axis=0), jnp.concatenate(idxs, axis=0)


def _take_rows(table, idx):
    out = jnp.zeros(idx.shape, table.dtype)
    for i in range(table.shape[0]):
        out = jnp.where(idx == i, table[i:i + 1, :], out)
    return out


def _peer_route_kernel(h_ref, wq_ref, sk_ref, e_ref, g_ref):
    hb = h_ref[...].astype(BF16)
    half = PEER_KEY_DIM // 2
    experts, gates = [], []
    for head in range(PEER_HEADS):
        top = []
        for p in range(2):
            c0 = head * PEER_KEY_DIM + p * half
            qhp = jnp.dot(hb, wq_ref[:, c0:c0 + half], preferred_element_type=F32)
            sc = _dot_t(sk_ref[p], qhp.astype(BF16))
            top.append(_top_rows(sc, PEER_TOPK))
        (v1, i1), (v2, i2) = top
        cand = jnp.concatenate([v1[i:i + 1, :] + v2 for i in range(PEER_TOPK)], axis=0)
        c_top, c_idx = _top_rows(cand, PEER_TOPK)
        e1 = _take_rows(i1, lax.shift_right_logical(c_idx, int(math.log2(PEER_TOPK))))
        e2 = _take_rows(i2, c_idx & (PEER_TOPK - 1))
        experts.append(e1 * N_SUBKEYS + e2)
        ex = jnp.exp(c_top - jnp.max(c_top, axis=0, keepdims=True))
        gates.append(ex / jnp.sum(ex, axis=0, keepdims=True))
    e_ref[...] = jnp.concatenate(experts, axis=0).T
    g_ref[...] = jnp.concatenate(gates, axis=0).T


def _peer_route(h, wq_bf16, subkeys_bf16):
    n = h.shape[0]
    tm = min(ROW_TILE, n)
    row = lambda i: (i, 0)
    return pl.pallas_call(
        _peer_route_kernel,
        grid=(n // tm,),
        in_specs=[pl.BlockSpec((tm, D_MODEL), row), _resident(wq_bf16.shape),
                  _resident(subkeys_bf16.shape)],
        out_specs=[pl.BlockSpec((tm, PEER_SLOTS), row)] * 2,
        out_shape=[jax.ShapeDtypeStruct((n, PEER_SLOTS), I32),
                   jax.ShapeDtypeStruct((n, PEER_SLOTS), F32)],
        compiler_params=_params("parallel"),
        name="peer_route",
    )(h, wq_bf16, subkeys_bf16)


PEER_TOKENS = 64


def _peer_expert_kernel(e_ref, h_ref, g_ref, uv_hbm, o_ref, buf, sem):
    tokens = h_ref.shape[0]
    eye = (lax.broadcasted_iota(I32, (PEER_SLOTS, PEER_SLOTS), 0)
           == lax.broadcasted_iota(I32, (PEER_SLOTS, PEER_SLOTS), 1))

    def start(t, slot):
        for k in range(PEER_SLOTS):
            e = e_ref[t, k]
            pltpu.make_async_copy(uv_hbm.at[pl.ds(e, 1)], buf.at[slot, pl.ds(k, 1)], sem.at[slot]).start()

    def wait(slot):
        pltpu.make_async_copy(uv_hbm.at[pl.ds(0, PEER_SLOTS)], buf.at[slot], sem.at[slot]).wait()

    start(0, 0)

    def token(t, carry):
        slot = t % 2

        @pl.when(t + 1 < tokens)
        def _():
            start(t + 1, 1 - slot)

        wait(slot)
        x = h_ref[pl.ds(t, 1), :]
        hid = jnp.sum(buf[slot, :, 0:D_MODEL] * x, axis=1, keepdims=True)
        gate = jnp.sum(jnp.where(eye, g_ref[pl.ds(t, 1), :], 0.0), axis=1, keepdims=True)
        a = gate * (0.5 * hid * (1.0 + lax.erf(hid * (2.0 ** -0.5))))
        o_ref[pl.ds(t, 1), :] = jnp.sum(a * buf[slot, :, D_MODEL:2 * D_MODEL], axis=0, keepdims=True)
        return carry

    lax.fori_loop(0, tokens, token, 0)


def _peer_expert(experts, h, gates, uv_tab):
    n = h.shape[0]
    tb = min(PEER_TOKENS, n)
    row = lambda i: (i, 0)
    return pl.pallas_call(
        _peer_expert_kernel,
        grid=(n // tb,),
        in_specs=[pl.BlockSpec((tb, PEER_SLOTS), row, memory_space=pltpu.SMEM),
                  pl.BlockSpec((tb, D_MODEL), row),
                  pl.BlockSpec((tb, PEER_SLOTS), row),
                  pl.BlockSpec(memory_space=pl.ANY)],
        out_specs=pl.BlockSpec((tb, D_MODEL), row),
        out_shape=jax.ShapeDtypeStruct((n, D_MODEL), F32),
        scratch_shapes=[pltpu.VMEM((2, PEER_SLOTS, 2 * D_MODEL), F32), pltpu.SemaphoreType.DMA((2,))],
        compiler_params=_params("arbitrary"),
        name="peer_expert",
    )(experts, h, gates, uv_tab)


def _tail_kernel(h_ref, y_ref, p_ref, g_ref, b_ref, wg_ref, wp_ref, o_ref):
    h2 = _layer_norm(DEEPNORM_ALPHA * h_ref[...] + y_ref[...], g_ref[...], b_ref[...])
    gate = jax.nn.sigmoid(jnp.dot(h2.astype(BF16), wg_ref[...], preferred_element_type=F32))
    emb = jnp.dot(p_ref[...].astype(BF16), wp_ref[...], preferred_element_type=F32)
    o_ref[...] = h2 + gate * emb


def _tail(h, y, p, g, b, wg_bf16, wp_bf16):
    n = h.shape[0]
    tm = min(ROW_TILE, n)
    row = lambda i: (i, 0)
    return pl.pallas_call(
        _tail_kernel,
        grid=(n // tm,),
        in_specs=[pl.BlockSpec((tm, D_MODEL), row), pl.BlockSpec((tm, D_MODEL), row),
                  pl.BlockSpec((tm, p.shape[1]), row), _resident(g.shape), _resident(b.shape),
                  _resident(wg_bf16.shape), _resident(wp_bf16.shape)],
        out_specs=pl.BlockSpec((tm, D_MODEL), row),
        out_shape=jax.ShapeDtypeStruct((n, D_MODEL), F32),
        compiler_params=_params("parallel"),
        name="tail",
    )(h, y, p, g, b, wg_bf16, wp_bf16)


def _rotary_tables(pos):
    half = HEAD_DIM // 2
    inv = ROPE_THETA ** (-jnp.arange(half, dtype=F32) / half)
    ang = pos.astype(F32)[:, None] * inv[None, :]
    cos, sin = jnp.cos(ang), jnp.sin(ang)
    return jnp.concatenate([cos, cos], axis=1), jnp.concatenate([-sin, sin], axis=1)


def _state_halo(state):
    return jnp.pad(state, ((0, 0), (SUBLANES - (CONV_WIDTH - 1), 0), (0, 0)))


def _layer_tail(x, att, conv, p, w):
    h1 = _out_proj(att, conv, x, w["w_out"], w["ln1_g"], w["ln1_b"])
    experts, gates = _peer_route(h1, w["peer_wq"], w["peer_subkeys"])
    y = _peer_expert(experts, h1, gates, w["peer_uv"])
    return _tail(h1, y, p, w["ln2_g"], w["ln2_b"], w["ple_gate"], w["ple_proj"])


def kernel(x_prompt, x_sample, state_attn_k, state_attn_v, state_conv, p_prompt, p_sample,
           w_in, w_conv, w_out, ln1_g, ln1_b, peer_wq, peer_subkeys, peer_u, peer_v,
           ln2_g, ln2_b, ple_gate, ple_proj):
    assert w_in.shape[0] == DEPTH
    bp, tp, _ = x_prompt.shape
    bs, ts, _ = x_sample.shape
    n_buf = state_attn_k.shape[2]
    w = {
        "w_in": w_in[0].astype(BF16), "w_out": w_out[0].astype(BF16),
        "ln1_g": ln1_g, "ln1_b": ln1_b, "ln2_g": ln2_g, "ln2_b": ln2_b,
        "peer_wq": peer_wq[0].astype(BF16), "peer_subkeys": peer_subkeys[0].astype(BF16),
        "peer_uv": jnp.concatenate([peer_u.reshape(-1, D_MODEL), peer_v.reshape(-1, D_MODEL)], axis=1),
        "ple_gate": ple_gate[0].astype(BF16), "ple_proj": ple_proj[0].astype(BF16),
    }
    wc = w_conv[0]

    xp = x_prompt.reshape(bp * tp, D_MODEL)
    cos_p, sin_p = _rotary_tables(jnp.arange(tp))
    q, k, v, z, gb = _in_proj(xp, w["w_in"], cos_p, sin_p)
    conv = _conv(z, gb, jnp.zeros((bp, SUBLANES, D_CONV), F32), wc, bp, tp)
    att = _attn_prompt(q, k, v, bp, tp)
    y_prompt = _layer_tail(xp, att, conv, p_prompt[0].reshape(bp * tp, -1), w).reshape(bp, tp, D_MODEL)
    new_k_prompt = k.reshape(1, bp, tp, ATT_HEADS, HEAD_DIM)
    new_v_prompt = v.reshape(1, bp, tp, ATT_HEADS, HEAD_DIM)
    new_conv_prompt = z.reshape(bp, tp, D_CONV)[:, tp - (CONV_WIDTH - 1):][None]

    xs = x_sample.reshape(bs * ts, D_MODEL)
    cos_s, sin_s = _rotary_tables(jnp.tile(PAST_LEN + jnp.arange(ts), bs))
    q, k, v, z, gb = _in_proj(xs, w["w_in"], cos_s, sin_s)
    conv = _conv(z, gb, _state_halo(state_conv[0]), wc, bs, ts)
    att, new_k, new_v = _attn_sample(q, k, v, state_attn_k[0].reshape(bs, n_buf, D_ATT),
                                     state_attn_v[0].reshape(bs, n_buf, D_ATT), bs, ts)
    y_sample = _layer_tail(xs, att, conv, p_sample[0].reshape(bs * ts, -1), w).reshape(bs, ts, D_MODEL)
    new_k_sample = new_k.reshape(1, bs, n_buf, ATT_HEADS, HEAD_DIM)
    new_v_sample = new_v.reshape(1, bs, n_buf, ATT_HEADS, HEAD_DIM)
    z_ext = jnp.concatenate([state_conv[0], z.reshape(bs, ts, D_CONV)], axis=1)
    new_conv_sample = z_ext[:, -(CONV_WIDTH - 1):][None]

    return (y_prompt, y_sample, new_k_prompt, new_v_prompt, new_conv_prompt,
            new_k_sample, new_v_sample, new_conv_sample)
```

```python
import functools
import math

import jax
import jax.numpy as jnp
from jax import lax
from jax.experimental import pallas as pl
from jax.experimental.pallas import tpu as pltpu

F32 = jnp.float32
BF16 = jnp.bfloat16
I32 = jnp.int32

D_MODEL = 2048
PAST_LEN = 16384
ATT_HEADS = 8
HEAD_DIM = 128
D_ATT = ATT_HEADS * HEAD_DIM
D_CONV = D_MODEL - D_ATT
CONV_WIDTH = 3
DILATIONS = (1, 4, 16)
BAND = 128
ROPE_THETA = 10000.0
PEER_HEADS = 8
N_SUBKEYS = 128
PEER_KEY_DIM = 256
PEER_TOPK = 16
PEER_SLOTS = PEER_HEADS * PEER_TOPK
DEPTH = 1
DEEPNORM_ALPHA = (2.0 * DEPTH) ** 0.25
LN_EPS = 1e-5

LANES = 128
SUBLANES = 8
ROW_TILE = 256
VMEM_LIMIT = 56 * 1024 * 1024
NEG_INF = float("-inf")


def _params(*sem):
    return pltpu.CompilerParams(dimension_semantics=sem, vmem_limit_bytes=VMEM_LIMIT)


def _resident(shape):
    nd = len(shape)
    return pl.BlockSpec(shape, lambda *_: (0,) * nd, pipeline_mode=pl.Buffered(1))


def _layer_norm(y, g, b):
    mu = jnp.mean(y, axis=-1, keepdims=True)
    c = y - mu
    var = jnp.mean(c * c, axis=-1, keepdims=True)
    return c * lax.rsqrt(var + LN_EPS) * g + b


def _in_proj_kernel(x_ref, w_ref, cos_ref, sin_ref, q_ref, k_ref, v_ref, z_ref, gb_ref):
    xb = x_ref[...].astype(BF16)
    cos = cos_ref[...]
    sin = sin_ref[...]

    def proj(c):
        return jnp.dot(xb, w_ref[:, c * D_ATT:(c + 1) * D_ATT], preferred_element_type=F32)

    def store_rotary(a, o_ref):
        for h in range(ATT_HEADS):
            ah = a[:, h * HEAD_DIM:(h + 1) * HEAD_DIM]
            o_ref[:, h * HEAD_DIM:(h + 1) * HEAD_DIM] = (
                ah * cos + pltpu.roll(ah, HEAD_DIM // 2, axis=1) * sin)

    store_rotary(proj(0), q_ref)
    store_rotary(proj(1), k_ref)
    v_ref[...] = proj(2)
    u = proj(3)
    gb_ref[...] = proj(4)
    z_ref[...] = proj(5) * u


def _in_proj(x, w_bf16, cos_tab, sin_tab):
    n = x.shape[0]
    tm = min(ROW_TILE, n)
    n_tab = cos_tab.shape[0] // tm
    row = lambda i: (i, 0)
    out = jax.ShapeDtypeStruct((n, D_ATT), F32)
    return pl.pallas_call(
        _in_proj_kernel,
        grid=(n // tm,),
        in_specs=[pl.BlockSpec((tm, D_MODEL), row),
                  _resident(w_bf16.shape),
                  pl.BlockSpec((tm, HEAD_DIM), lambda i: (i % n_tab, 0)),
                  pl.BlockSpec((tm, HEAD_DIM), lambda i: (i % n_tab, 0))],
        out_specs=[pl.BlockSpec((tm, D_ATT), row)] * 5,
        out_shape=[out] * 5,
        compiler_params=_params("parallel"),
        name="in_proj",
    )(x, w_bf16, cos_tab, sin_tab)


def _conv_kernel(z_ref, zprev_ref, st_ref, gb_ref, wc_ref, o_ref):
    halo = jnp.where(pl.program_id(1) == 0, st_ref[...], zprev_ref[...])
    z = z_ref[...]
    row = lax.broadcasted_iota(I32, z.shape, 0)
    h1 = halo[SUBLANES - 1:SUBLANES, :]
    h2 = halo[SUBLANES - 2:SUBLANES - 1, :]
    zm1 = jnp.where(row == 0, h1, pltpu.roll(z, 1, axis=0))
    zm2 = jnp.where(row == 0, h2, jnp.where(row == 1, h1, pltpu.roll(z, 2, axis=0)))
    wc = wc_ref[...]
    o_ref[...] = gb_ref[...] * (wc[0:1, :] * zm2 + wc[1:2, :] * zm1 + wc[2:3, :] * z)


def _conv(z, gb, state_halo, w_conv, batch, seq):
    tt = min(ROW_TILE, seq)
    per_seq = seq // tt
    halo_per_tile = tt // SUBLANES
    z3 = z.reshape(batch, seq, D_CONV)
    gb3 = gb.reshape(batch, seq, D_CONV)
    tile = pl.BlockSpec((None, tt, D_CONV), lambda b, i: (b, i, 0))
    out = pl.pallas_call(
        _conv_kernel,
        grid=(batch, per_seq),
        in_specs=[tile,
                  pl.BlockSpec((None, SUBLANES, D_CONV),
                               lambda b, i: (b, jnp.maximum(i * halo_per_tile - 1, 0), 0)),
                  pl.BlockSpec((None, SUBLANES, D_CONV), lambda b, i: (b, 0, 0)),
                  tile,
                  pl.BlockSpec((CONV_WIDTH, D_CONV), lambda b, i: (0, 0))],
        out_specs=tile,
        out_shape=jax.ShapeDtypeStruct((batch, seq, D_CONV), F32),
        compiler_params=_params("parallel", "parallel"),
        name="conv",
    )(z3, z3, state_halo, gb3, w_conv)
    return out.reshape(batch * seq, D_CONV)


def _dot_t(a, b):
    return lax.dot_general(a, b, (((1,), (1,)), ((), ())), preferred_element_type=F32)


ATTN_UNROLL = 4


def _attn_prompt_kernel(q_ref, k_ref, v_ref, o_ref, acc_sc, m_sc, l_sc):
    seq = q_ref.shape[0]
    scale = HEAD_DIM ** -0.5
    qi = lax.broadcasted_iota(I32, (BAND, BAND), 0)
    kj = lax.broadcasted_iota(I32, (BAND, BAND), 1)
    cur_ok = kj <= qi
    prev_ok = kj >= qi

    for bi, d in enumerate(DILATIONS):
        nb = seq // (d * BAND)

        def block(blk, carry, bi=bi, d=d, nb=nb):
            r = blk // nb
            n = blk % nb
            start = r + n * (d * BAND)
            prev_start = jnp.maximum(start - d * BAND, 0)
            rows = pl.ds(start, BAND, stride=d) if d > 1 else pl.ds(pl.multiple_of(start, BAND), BAND)
            prows = pl.ds(prev_start, BAND, stride=d) if d > 1 else pl.ds(pl.multiple_of(prev_start, BAND), BAND)
            q = q_ref[rows, :].astype(BF16)
            s_cur = _dot_t(q, k_ref[rows, :].astype(BF16)) * scale
            s_prev = _dot_t(q, k_ref[prows, :].astype(BF16)) * scale
            s_cur = jnp.where(cur_ok, s_cur, NEG_INF)
            s_prev = jnp.where(prev_ok, s_prev, NEG_INF) + jnp.where(n > 0, 0.0, NEG_INF)
            m = jnp.maximum(jnp.max(s_cur, axis=1, keepdims=True),
                            jnp.max(s_prev, axis=1, keepdims=True))
            p_cur = jnp.exp(s_cur - m)
            p_prev = jnp.exp(s_prev - m)
            l = jnp.sum(p_cur, axis=1, keepdims=True) + jnp.sum(p_prev, axis=1, keepdims=True)
            acc = (jnp.dot(p_cur.astype(BF16), v_ref[rows, :].astype(BF16), preferred_element_type=F32)
                   + jnp.dot(p_prev.astype(BF16), v_ref[prows, :].astype(BF16), preferred_element_type=F32))
            acc_sc[bi, rows, :] = acc
            m_sc[bi, rows, :] = jnp.broadcast_to(m, (BAND, HEAD_DIM))
            l_sc[bi, rows, :] = jnp.broadcast_to(l, (BAND, HEAD_DIM))
            return carry

        lax.fori_loop(0, d * nb, block, 0, unroll=ATTN_UNROLL)

    big_m = jnp.maximum(jnp.maximum(m_sc[0], m_sc[1]), m_sc[2])
    num = jnp.zeros((seq, HEAD_DIM), F32)
    den = jnp.zeros((seq, HEAD_DIM), F32)
    for bi in range(len(DILATIONS)):
        w = jnp.exp(m_sc[bi] - big_m)
        num = num + w * acc_sc[bi]
        den = den + w * l_sc[bi]
    o_ref[...] = num / den


def _attn_prompt(q, k, v, batch, seq):
    q3, k3, v3 = (a.reshape(batch, seq, D_ATT) for a in (q, k, v))
    head = pl.BlockSpec((None, seq, HEAD_DIM), lambda b, h: (b, 0, h))
    nbr = len(DILATIONS)
    out = pl.pallas_call(
        _attn_prompt_kernel,
        grid=(batch, ATT_HEADS),
        in_specs=[head, head, head],
        out_specs=head,
        out_shape=jax.ShapeDtypeStruct((batch, seq, D_ATT), F32),
        scratch_shapes=[pltpu.VMEM((nbr, seq, HEAD_DIM), F32)] * 3,
        compiler_params=_params("parallel", "parallel"),
        name="attn_prompt",
    )(q3, k3, v3)
    return out.reshape(batch * seq, D_ATT)


def _branch_count(rel):
    cnt = jnp.zeros(rel.shape, F32)
    for d in DILATIONS:
        hit = (rel >= 0) & (rel <= d * BAND) & ((rel & (d - 1)) == 0)
        cnt = cnt + jnp.where(hit, 1.0, 0.0)
    return cnt


def _attn_sample_kernel(q_ref, kn_ref, vn_ref, ks_ref, vs_ref, o_ref, ko_ref, vo_ref):
    n_buf = ks_ref.shape[0]
    t_new = q_ref.shape[0]
    scale = HEAD_DIM ** -0.5
    q = q_ref[...].astype(BF16)
    ks = ks_ref[...]
    vs = vs_ref[...]
    kn = kn_ref[...]
    vn = vn_ref[...]
    s_st = _dot_t(q, ks.astype(BF16)) * scale
    s_nw = _dot_t(q, kn.astype(BF16)) * scale
    qpos = n_buf + lax.broadcasted_iota(I32, s_st.shape, 0)
    c_st = _branch_count(qpos - lax.broadcasted_iota(I32, s_st.shape, 1))
    c_nw = _branch_count(lax.broadcasted_iota(I32, s_nw.shape, 0)
                         - lax.broadcasted_iota(I32, s_nw.shape, 1))
    s_st = jnp.where(c_st > 0, s_st, NEG_INF)
    s_nw = jnp.where(c_nw > 0, s_nw, NEG_INF)
    m = jnp.maximum(jnp.max(s_st, axis=1, keepdims=True), jnp.max(s_nw, axis=1, keepdims=True))
    p_st = c_st * jnp.exp(s_st - m)
    p_nw = c_nw * jnp.exp(s_nw - m)
    l = jnp.sum(p_st, axis=1, keepdims=True) + jnp.sum(p_nw, axis=1, keepdims=True)
    acc = (jnp.dot(p_st.astype(BF16), vs.astype(BF16), preferred_element_type=F32)
           + jnp.dot(p_nw.astype(BF16), vn.astype(BF16), preferred_element_type=F32))
    o_ref[...] = acc / l
    keep = n_buf - t_new
    ko_ref[0:keep, :] = ks[t_new:, :]
    ko_ref[keep:, :] = kn
    vo_ref[0:keep, :] = vs[t_new:, :]
    vo_ref[keep:, :] = vn


def _attn_sample(q, k_new, v_new, k_state, v_state, batch, t_new):
    n_buf = k_state.shape[1]
    new = pl.BlockSpec((t_new, HEAD_DIM), lambda b, h: (b, h))
    buf = pl.BlockSpec((None, n_buf, HEAD_DIM), lambda b, h: (b, 0, h))
    buf_shape = jax.ShapeDtypeStruct((batch, n_buf, D_ATT), F32)
    return pl.pallas_call(
        _attn_sample_kernel,
        grid=(batch, ATT_HEADS),
        in_specs=[new, new, new, buf, buf],
        out_specs=[new, buf, buf],
        out_shape=[jax.ShapeDtypeStruct((batch * t_new, D_ATT), F32), buf_shape, buf_shape],
        compiler_params=_params("parallel", "parallel"),
        name="attn_sample",
    )(q, k_new, v_new, k_state, v_state)


def _out_proj_kernel(att_ref, conv_ref, x_ref, w_ref, g_ref, b_ref, o_ref):
    mixed = (jnp.dot(att_ref[...].astype(BF16), w_ref[0:D_ATT, :], preferred_element_type=F32)
             + jnp.dot(conv_ref[...].astype(BF16), w_ref[D_ATT:D_MODEL, :], preferred_element_type=F32))
    o_ref[...] = _layer_norm(DEEPNORM_ALPHA * x_ref[...] + mixed, g_ref[...], b_ref[...])


def _out_proj(att, conv, x, w_bf16, g, b):
    n = x.shape[0]
    tm = min(ROW_TILE, n)
    row = lambda i: (i, 0)
    return pl.pallas_call(
        _out_proj_kernel,
        grid=(n // tm,),
        in_specs=[pl.BlockSpec((tm, D_ATT), row), pl.BlockSpec((tm, D_CONV), row),
                  pl.BlockSpec((tm, D_MODEL), row), _resident(w_bf16.shape),
                  _resident(g.shape), _resident(b.shape)],
        out_specs=pl.BlockSpec((tm, D_MODEL), row),
        out_shape=jax.ShapeDtypeStruct((n, D_MODEL), F32),
        compiler_params=_params("parallel"),
        name="out_proj",
    )(att, conv, x, w_bf16, g, b)


ID_NONE = float(2 ** 24)


def _top_rows(s, ids, k):
    vals, sel = [], []
    for _ in range(k):
        m = jnp.max(s, axis=0, keepdims=True)
        i = jnp.min(jnp.where(s == m, ids, ID_NONE), axis=0, keepdims=True)
        vals.append(m)
        sel.append(i)
        s = jnp.where(ids == i, NEG_INF, s)
    return jnp.concatenate(vals, axis=0), jnp.concatenate(sel, axis=0)


def _pair_candidates(v1, v2, sub):
    assert PEER_TOPK == 2 * SUBLANES
    vals = [v1[0:1, :] + v2]
    ids = [sub, sub + SUBLANES]
    for i in range(1, SUBLANES):
        vals.append(v1[i:i + 1, :] + v2[0:SUBLANES, :])
        ids.append(sub + i * PEER_TOPK)
    vals.append(v1[SUBLANES:, :] + v2[0:1, :])
    ids.append((sub + SUBLANES) * PEER_TOPK)
    return jnp.concatenate(vals, axis=0), jnp.concatenate(ids, axis=0)


def _take_rows(table, idx):
    out = jnp.zeros(idx.shape, table.dtype)
    for i in range(table.shape[0]):
        out = jnp.where(idx == i, table[i:i + 1, :], out)
    return out


def _peer_route_kernel(h_ref, wq_ref, sk_ref, e_ref, g_ref):
    hb = h_ref[...].astype(BF16)
    tokens = hb.shape[0]
    half = PEER_KEY_DIM // 2
    key_ids = lax.broadcasted_iota(I32, (N_SUBKEYS, tokens), 0).astype(F32)
    sub = lax.broadcasted_iota(I32, (SUBLANES, tokens), 0).astype(F32)
    experts, gates = [], []
    for head in range(PEER_HEADS):
        top = []
        for p in range(2):
            c0 = head * PEER_KEY_DIM + p * half
            qhp = jnp.dot(hb, wq_ref[:, c0:c0 + half], preferred_element_type=F32)
            sc = _dot_t(sk_ref[p], qhp.astype(BF16))
            top.append(_top_rows(sc, key_ids, PEER_TOPK))
        (v1, i1), (v2, i2) = top
        cand, cand_ids = _pair_candidates(v1, v2, sub)
        c_top, c_id = _top_rows(cand, cand_ids, PEER_TOPK)
        rank1 = jnp.floor(c_id * (1.0 / PEER_TOPK))
        rank2 = c_id - rank1 * PEER_TOPK
        experts.append(_take_rows(i1, rank1) * N_SUBKEYS + _take_rows(i2, rank2))
        ex = jnp.exp(c_top - jnp.max(c_top, axis=0, keepdims=True))
        gates.append(ex / jnp.sum(ex, axis=0, keepdims=True))
    e_ref[...] = jnp.concatenate(experts, axis=0).T.astype(I32)
    g_ref[...] = jnp.concatenate(gates, axis=0).T


def _peer_route(h, wq_bf16, subkeys_bf16):
    n = h.shape[0]
    tm = min(ROW_TILE, n)
    row = lambda i: (i, 0)
    return pl.pallas_call(
        _peer_route_kernel,
        grid=(n // tm,),
        in_specs=[pl.BlockSpec((tm, D_MODEL), row), _resident(wq_bf16.shape),
                  _resident(subkeys_bf16.shape)],
        out_specs=[pl.BlockSpec((tm, PEER_SLOTS), row)] * 2,
        out_shape=[jax.ShapeDtypeStruct((n, PEER_SLOTS), I32),
                   jax.ShapeDtypeStruct((n, PEER_SLOTS), F32)],
        compiler_params=_params("parallel"),
        name="peer_route",
    )(h, wq_bf16, subkeys_bf16)


PEER_TOKENS = 64
FOLD = D_MODEL // LANES


def _peer_expert_kernel(e_ref, h_ref, g_ref, uv_hbm, o_ref, buf, sem):
    tokens = h_ref.shape[0]
    eye = (lax.broadcasted_iota(I32, (PEER_SLOTS, PEER_SLOTS), 0)
           == lax.broadcasted_iota(I32, (PEER_SLOTS, PEER_SLOTS), 1))

    def start(t, slot):
        for k in range(PEER_SLOTS):
            pltpu.make_async_copy(uv_hbm.at[e_ref[t, k]], buf.at[slot, k], sem.at[slot]).start()

    def wait(slot):
        pltpu.make_async_copy(uv_hbm.at[pl.ds(0, PEER_SLOTS)], buf.at[slot], sem.at[slot]).wait()

    start(0, 0)

    def token(t, carry):
        slot = t % 2

        @pl.when(t + 1 < tokens)
        def _():
            start(t + 1, 1 - slot)

        wait(slot)
        xrow = h_ref[pl.ds(t, 1), :]
        x = jnp.concatenate([xrow[:, j * LANES:(j + 1) * LANES] for j in range(FOLD)], axis=0)
        prod = buf[slot, :, 0:FOLD, :] * x[None]
        hid = jnp.sum(jnp.sum(prod, axis=1), axis=1, keepdims=True)
        gate = jnp.sum(jnp.where(eye, g_ref[pl.ds(t, 1), :], 0.0), axis=1, keepdims=True)
        a = gate * (0.5 * hid * (1.0 + lax.erf(hid * (2.0 ** -0.5))))
        a = jnp.broadcast_to(a, (PEER_SLOTS, LANES))[:, None, :]
        y = jnp.sum(a * buf[slot, :, FOLD:2 * FOLD, :], axis=0)
        o_ref[pl.ds(t, 1), :] = jnp.concatenate([y[j:j + 1, :] for j in range(FOLD)], axis=1)
        return carry

    lax.fori_loop(0, tokens, token, 0)


def _peer_expert(experts, h, gates, uv_tab):
    n = h.shape[0]
    tb = min(PEER_TOKENS, n)
    row = lambda i: (i, 0)
    return pl.pallas_call(
        _peer_expert_kernel,
        grid=(n // tb,),
        in_specs=[pl.BlockSpec((tb, PEER_SLOTS), row, memory_space=pltpu.SMEM),
                  pl.BlockSpec((tb, D_MODEL), row),
                  pl.BlockSpec((tb, PEER_SLOTS), row),
                  pl.BlockSpec(memory_space=pl.ANY)],
        out_specs=pl.BlockSpec((tb, D_MODEL), row),
        out_shape=jax.ShapeDtypeStruct((n, D_MODEL), F32),
        scratch_shapes=[pltpu.VMEM((2, PEER_SLOTS, 2 * FOLD, LANES), F32), pltpu.SemaphoreType.DMA((2,))],
        compiler_params=_params("arbitrary"),
        name="peer_expert",
    )(experts, h, gates, uv_tab)


def _tail_kernel(h_ref, y_ref, p_ref, g_ref, b_ref, wg_ref, wp_ref, o_ref):
    h2 = _layer_norm(DEEPNORM_ALPHA * h_ref[...] + y_ref[...], g_ref[...], b_ref[...])
    gate = jax.nn.sigmoid(jnp.dot(h2.astype(BF16), wg_ref[...], preferred_element_type=F32))
    emb = jnp.dot(p_ref[...].astype(BF16), wp_ref[...], preferred_element_type=F32)
    o_ref[...] = h2 + gate * emb


def _tail(h, y, p, g, b, wg_bf16, wp_bf16):
    n = h.shape[0]
    tm = min(ROW_TILE, n)
    row = lambda i: (i, 0)
    return pl.pallas_call(
        _tail_kernel,
        grid=(n // tm,),
        in_specs=[pl.BlockSpec((tm, D_MODEL), row), pl.BlockSpec((tm, D_MODEL), row),
                  pl.BlockSpec((tm, p.shape[1]), row), _resident(g.shape), _resident(b.shape),
                  _resident(wg_bf16.shape), _resident(wp_bf16.shape)],
        out_specs=pl.BlockSpec((tm, D_MODEL), row),
        out_shape=jax.ShapeDtypeStruct((n, D_MODEL), F32),
        compiler_params=_params("parallel"),
        name="tail",
    )(h, y, p, g, b, wg_bf16, wp_bf16)


def _rotary_tables(pos):
    half = HEAD_DIM // 2
    inv = ROPE_THETA ** (-jnp.arange(half, dtype=F32) / half)
    ang = pos.astype(F32)[:, None] * inv[None, :]
    cos, sin = jnp.cos(ang), jnp.sin(ang)
    return jnp.concatenate([cos, cos], axis=1), jnp.concatenate([-sin, sin], axis=1)


def _state_halo(state):
    return jnp.pad(state, ((0, 0), (SUBLANES - (CONV_WIDTH - 1), 0), (0, 0)))


def _layer_tail(x, att, conv, p, w):
    h1 = _out_proj(att, conv, x, w["w_out"], w["ln1_g"], w["ln1_b"])
    experts, gates = _peer_route(h1, w["peer_wq"], w["peer_subkeys"])
    y = _peer_expert(experts, h1, gates, w["peer_uv"])
    return _tail(h1, y, p, w["ln2_g"], w["ln2_b"], w["ple_gate"], w["ple_proj"])


def kernel(x_prompt, x_sample, state_attn_k, state_attn_v, state_conv, p_prompt, p_sample,
           w_in, w_conv, w_out, ln1_g, ln1_b, peer_wq, peer_subkeys, peer_u, peer_v,
           ln2_g, ln2_b, ple_gate, ple_proj):
    assert w_in.shape[0] == DEPTH
    bp, tp, _ = x_prompt.shape
    bs, ts, _ = x_sample.shape
    n_buf = state_attn_k.shape[2]
    w = {
        "w_in": w_in[0].astype(BF16), "w_out": w_out[0].astype(BF16),
        "ln1_g": ln1_g, "ln1_b": ln1_b, "ln2_g": ln2_g, "ln2_b": ln2_b,
        "peer_wq": peer_wq[0].astype(BF16), "peer_subkeys": peer_subkeys[0].astype(BF16),
        "peer_uv": jnp.concatenate([peer_u.reshape(-1, FOLD, LANES), peer_v.reshape(-1, FOLD, LANES)], axis=1),
        "ple_gate": ple_gate[0].astype(BF16), "ple_proj": ple_proj[0].astype(BF16),
    }
    wc = w_conv[0]

    xp = x_prompt.reshape(bp * tp, D_MODEL)
    cos_p, sin_p = _rotary_tables(jnp.arange(tp))
    q, k, v, z, gb = _in_proj(xp, w["w_in"], cos_p, sin_p)
    conv = _conv(z, gb, jnp.zeros((bp, SUBLANES, D_CONV), F32), wc, bp, tp)
    att = _attn_prompt(q, k, v, bp, tp)
    y_prompt = _layer_tail(xp, att, conv, p_prompt[0].reshape(bp * tp, -1), w).reshape(bp, tp, D_MODEL)
    new_k_prompt = k.reshape(1, bp, tp, ATT_HEADS, HEAD_DIM)
    new_v_prompt = v.reshape(1, bp, tp, ATT_HEADS, HEAD_DIM)
    new_conv_prompt = z.reshape(bp, tp, D_CONV)[:, tp - (CONV_WIDTH - 1):][None]

    xs = x_sample.reshape(bs * ts, D_MODEL)
    cos_s, sin_s = _rotary_tables(jnp.tile(PAST_LEN + jnp.arange(ts), bs))
    q, k, v, z, gb = _in_proj(xs, w["w_in"], cos_s, sin_s)
    conv = _conv(z, gb, _state_halo(state_conv[0]), wc, bs, ts)
    att, new_k, new_v = _attn_sample(q, k, v, state_attn_k[0].reshape(bs, n_buf, D_ATT),
                                     state_attn_v[0].reshape(bs, n_buf, D_ATT), bs, ts)
    y_sample = _layer_tail(xs, att, conv, p_sample[0].reshape(bs * ts, -1), w).reshape(bs, ts, D_MODEL)
    new_k_sample = new_k.reshape(1, bs, n_buf, ATT_HEADS, HEAD_DIM)
    new_v_sample = new_v.reshape(1, bs, n_buf, ATT_HEADS, HEAD_DIM)
    z_ext = jnp.concatenate([state_conv[0], z.reshape(bs, ts, D_CONV)], axis=1)
    new_conv_sample = z_ext[:, -(CONV_WIDTH - 1):][None]

    return (y_prompt, y_sample, new_k_prompt, new_v_prompt, new_conv_prompt,
            new_k_sample, new_v_sample, new_conv_sample)
```

```python
import functools
import math

import jax
import jax.numpy as jnp
from jax import lax
from jax.experimental import pallas as pl
from jax.experimental.pallas import tpu as pltpu

F32 = jnp.float32
BF16 = jnp.bfloat16
I32 = jnp.int32

D_MODEL = 2048
PAST_LEN = 16384
ATT_HEADS = 8
HEAD_DIM = 128
D_ATT = ATT_HEADS * HEAD_DIM
D_CONV = D_MODEL - D_ATT
CONV_WIDTH = 3
DILATIONS = (1, 4, 16)
BAND = 128
ROPE_THETA = 10000.0
PEER_HEADS = 8
N_SUBKEYS = 128
PEER_KEY_DIM = 256
PEER_TOPK = 16
PEER_SLOTS = PEER_HEADS * PEER_TOPK
DEPTH = 1
DEEPNORM_ALPHA = (2.0 * DEPTH) ** 0.25
LN_EPS = 1e-5

LANES = 128
SUBLANES = 8
ROW_TILE = 256
VMEM_LIMIT = 56 * 1024 * 1024
NEG_INF = float("-inf")


def _params(*sem):
    return pltpu.CompilerParams(dimension_semantics=sem, vmem_limit_bytes=VMEM_LIMIT)


def _resident(shape):
    nd = len(shape)
    return pl.BlockSpec(shape, lambda *_: (0,) * nd, pipeline_mode=pl.Buffered(1))


def _layer_norm(y, g, b):
    mu = jnp.mean(y, axis=-1, keepdims=True)
    c = y - mu
    var = jnp.mean(c * c, axis=-1, keepdims=True)
    return c * lax.rsqrt(var + LN_EPS) * g + b


def _in_proj_kernel(x_ref, w_ref, cos_ref, sin_ref, q_ref, k_ref, v_ref, z_ref, gb_ref):
    xb = x_ref[...].astype(BF16)
    cos = cos_ref[...]
    sin = sin_ref[...]

    def proj(c):
        return jnp.dot(xb, w_ref[:, c * D_ATT:(c + 1) * D_ATT], preferred_element_type=F32)

    def store_rotary(a, o_ref):
        for h in range(ATT_HEADS):
            ah = a[:, h * HEAD_DIM:(h + 1) * HEAD_DIM]
            o_ref[:, h * HEAD_DIM:(h + 1) * HEAD_DIM] = (
                ah * cos + pltpu.roll(ah, HEAD_DIM // 2, axis=1) * sin)

    store_rotary(proj(0), q_ref)
    store_rotary(proj(1), k_ref)
    v_ref[...] = proj(2)
    u = proj(3)
    gb_ref[...] = proj(4)
    z_ref[...] = proj(5) * u


def _in_proj(x, w_bf16, cos_tab, sin_tab):
    n = x.shape[0]
    tm = min(ROW_TILE, n)
    n_tab = cos_tab.shape[0] // tm
    row = lambda i: (i, 0)
    out = jax.ShapeDtypeStruct((n, D_ATT), F32)
    return pl.pallas_call(
        _in_proj_kernel,
        grid=(n // tm,),
        in_specs=[pl.BlockSpec((tm, D_MODEL), row),
                  _resident(w_bf16.shape),
                  pl.BlockSpec((tm, HEAD_DIM), lambda i: (i % n_tab, 0)),
                  pl.BlockSpec((tm, HEAD_DIM), lambda i: (i % n_tab, 0))],
        out_specs=[pl.BlockSpec((tm, D_ATT), row)] * 5,
        out_shape=[out] * 5,
        compiler_params=_params("parallel"),
        name="in_proj",
    )(x, w_bf16, cos_tab, sin_tab)


def _conv_kernel(z_ref, zprev_ref, st_ref, gb_ref, wc_ref, o_ref):
    halo = jnp.where(pl.program_id(1) == 0, st_ref[...], zprev_ref[...])
    z = z_ref[...]
    row = lax.broadcasted_iota(I32, z.shape, 0)
    h1 = halo[SUBLANES - 1:SUBLANES, :]
    h2 = halo[SUBLANES - 2:SUBLANES - 1, :]
    zm1 = jnp.where(row == 0, h1, pltpu.roll(z, 1, axis=0))
    zm2 = jnp.where(row == 0, h2, jnp.where(row == 1, h1, pltpu.roll(z, 2, axis=0)))
    wc = wc_ref[...]
    o_ref[...] = gb_ref[...] * (wc[0:1, :] * zm2 + wc[1:2, :] * zm1 + wc[2:3, :] * z)


def _conv(z, gb, state_halo, w_conv, batch, seq):
    tt = min(ROW_TILE, seq)
    per_seq = seq // tt
    halo_per_tile = tt // SUBLANES
    z3 = z.reshape(batch, seq, D_CONV)
    gb3 = gb.reshape(batch, seq, D_CONV)
    tile = pl.BlockSpec((None, tt, D_CONV), lambda b, i: (b, i, 0))
    out = pl.pallas_call(
        _conv_kernel,
        grid=(batch, per_seq),
        in_specs=[tile,
                  pl.BlockSpec((None, SUBLANES, D_CONV),
                               lambda b, i: (b, jnp.maximum(i * halo_per_tile - 1, 0), 0)),
                  pl.BlockSpec((None, SUBLANES, D_CONV), lambda b, i: (b, 0, 0)),
                  tile,
                  pl.BlockSpec((CONV_WIDTH, D_CONV), lambda b, i: (0, 0))],
        out_specs=tile,
        out_shape=jax.ShapeDtypeStruct((batch, seq, D_CONV), F32),
        compiler_params=_params("parallel", "parallel"),
        name="conv",
    )(z3, z3, state_halo, gb3, w_conv)
    return out.reshape(batch * seq, D_CONV)


def _dot_t(a, b):
    return lax.dot_general(a, b, (((1,), (1,)), ((), ())), preferred_element_type=F32)


ATTN_UNROLL = 4


def _attn_prompt_kernel(q_ref, k_ref, v_ref, o_ref, acc_sc, m_sc, l_sc):
    seq = q_ref.shape[0]
    scale = HEAD_DIM ** -0.5
    qi = lax.broadcasted_iota(I32, (BAND, BAND), 0)
    kj = lax.broadcasted_iota(I32, (BAND, BAND), 1)
    cur_ok = kj <= qi
    prev_ok = kj >= qi

    for bi, d in enumerate(DILATIONS):
        nb = seq // (d * BAND)

        def block(blk, carry, bi=bi, d=d, nb=nb):
            r = blk // nb
            n = blk % nb
            start = r + n * (d * BAND)
            prev_start = jnp.maximum(start - d * BAND, 0)
            rows = pl.ds(start, BAND, stride=d) if d > 1 else pl.ds(pl.multiple_of(start, BAND), BAND)
            prows = pl.ds(prev_start, BAND, stride=d) if d > 1 else pl.ds(pl.multiple_of(prev_start, BAND), BAND)
            q = q_ref[rows, :].astype(BF16)
            s_cur = _dot_t(q, k_ref[rows, :].astype(BF16)) * scale
            s_prev = _dot_t(q, k_ref[prows, :].astype(BF16)) * scale
            s_cur = jnp.where(cur_ok, s_cur, NEG_INF)
            s_prev = jnp.where(prev_ok, s_prev, NEG_INF) + jnp.where(n > 0, 0.0, NEG_INF)
            m = jnp.maximum(jnp.max(s_cur, axis=1, keepdims=True),
                            jnp.max(s_prev, axis=1, keepdims=True))
            p_cur = jnp.exp(s_cur - m)
            p_prev = jnp.exp(s_prev - m)
            l = jnp.sum(p_cur, axis=1, keepdims=True) + jnp.sum(p_prev, axis=1, keepdims=True)
            acc = (jnp.dot(p_cur.astype(BF16), v_ref[rows, :].astype(BF16), preferred_element_type=F32)
                   + jnp.dot(p_prev.astype(BF16), v_ref[prows, :].astype(BF16), preferred_element_type=F32))
            acc_sc[bi, rows, :] = acc
            m_sc[bi, rows, :] = jnp.broadcast_to(m, (BAND, HEAD_DIM))
            l_sc[bi, rows, :] = jnp.broadcast_to(l, (BAND, HEAD_DIM))
            return carry

        lax.fori_loop(0, d * nb, block, 0, unroll=ATTN_UNROLL)

    big_m = jnp.maximum(jnp.maximum(m_sc[0], m_sc[1]), m_sc[2])
    num = jnp.zeros((seq, HEAD_DIM), F32)
    den = jnp.zeros((seq, HEAD_DIM), F32)
    for bi in range(len(DILATIONS)):
        w = jnp.exp(m_sc[bi] - big_m)
        num = num + w * acc_sc[bi]
        den = den + w * l_sc[bi]
    o_ref[...] = num / den


def _attn_prompt(q, k, v, batch, seq):
    q3, k3, v3 = (a.reshape(batch, seq, D_ATT) for a in (q, k, v))
    head = pl.BlockSpec((None, seq, HEAD_DIM), lambda b, h: (b, 0, h))
    nbr = len(DILATIONS)
    out = pl.pallas_call(
        _attn_prompt_kernel,
        grid=(batch, ATT_HEADS),
        in_specs=[head, head, head],
        out_specs=head,
        out_shape=jax.ShapeDtypeStruct((batch, seq, D_ATT), F32),
        scratch_shapes=[pltpu.VMEM((nbr, seq, HEAD_DIM), F32)] * 3,
        compiler_params=_params("parallel", "parallel"),
        name="attn_prompt",
    )(q3, k3, v3)
    return out.reshape(batch * seq, D_ATT)


def _branch_count(rel):
    cnt = jnp.zeros(rel.shape, F32)
    for d in DILATIONS:
        hit = (rel >= 0) & (rel <= d * BAND) & ((rel & (d - 1)) == 0)
        cnt = cnt + jnp.where(hit, 1.0, 0.0)
    return cnt


def _attn_sample_kernel(q_ref, kn_ref, vn_ref, ks_ref, vs_ref, o_ref, ko_ref, vo_ref):
    n_buf = ks_ref.shape[0]
    t_new = q_ref.shape[0]
    scale = HEAD_DIM ** -0.5
    q = q_ref[...].astype(BF16)
    ks = ks_ref[...]
    vs = vs_ref[...]
    kn = kn_ref[...]
    vn = vn_ref[...]
    s_st = _dot_t(q, ks.astype(BF16)) * scale
    s_nw = _dot_t(q, kn.astype(BF16)) * scale
    qpos = n_buf + lax.broadcasted_iota(I32, s_st.shape, 0)
    c_st = _branch_count(qpos - lax.broadcasted_iota(I32, s_st.shape, 1))
    c_nw = _branch_count(lax.broadcasted_iota(I32, s_nw.shape, 0)
                         - lax.broadcasted_iota(I32, s_nw.shape, 1))
    s_st = jnp.where(c_st > 0, s_st, NEG_INF)
    s_nw = jnp.where(c_nw > 0, s_nw, NEG_INF)
    m = jnp.maximum(jnp.max(s_st, axis=1, keepdims=True), jnp.max(s_nw, axis=1, keepdims=True))
    p_st = c_st * jnp.exp(s_st - m)
    p_nw = c_nw * jnp.exp(s_nw - m)
    l = jnp.sum(p_st, axis=1, keepdims=True) + jnp.sum(p_nw, axis=1, keepdims=True)
    acc = (jnp.dot(p_st.astype(BF16), vs.astype(BF16), preferred_element_type=F32)
           + jnp.dot(p_nw.astype(BF16), vn.astype(BF16), preferred_element_type=F32))
    o_ref[...] = acc / l
    keep = n_buf - t_new
    ko_ref[0:keep, :] = ks[t_new:, :]
    ko_ref[keep:, :] = kn
    vo_ref[0:keep, :] = vs[t_new:, :]
    vo_ref[keep:, :] = vn


def _attn_sample(q, k_new, v_new, k_state, v_state, batch, t_new):
    n_buf = k_state.shape[1]
    new = pl.BlockSpec((t_new, HEAD_DIM), lambda b, h: (b, h))
    buf = pl.BlockSpec((None, n_buf, HEAD_DIM), lambda b, h: (b, 0, h))
    buf_shape = jax.ShapeDtypeStruct((batch, n_buf, D_ATT), F32)
    return pl.pallas_call(
        _attn_sample_kernel,
        grid=(batch, ATT_HEADS),
        in_specs=[new, new, new, buf, buf],
        out_specs=[new, buf, buf],
        out_shape=[jax.ShapeDtypeStruct((batch * t_new, D_ATT), F32), buf_shape, buf_shape],
        compiler_params=_params("parallel", "parallel"),
        name="attn_sample",
    )(q, k_new, v_new, k_state, v_state)


def _out_proj_kernel(att_ref, conv_ref, x_ref, w_ref, g_ref, b_ref, o_ref):
    mixed = (jnp.dot(att_ref[...].astype(BF16), w_ref[0:D_ATT, :], preferred_element_type=F32)
             + jnp.dot(conv_ref[...].astype(BF16), w_ref[D_ATT:D_MODEL, :], preferred_element_type=F32))
    o_ref[...] = _layer_norm(DEEPNORM_ALPHA * x_ref[...] + mixed, g_ref[...], b_ref[...])


def _out_proj(att, conv, x, w_bf16, g, b):
    n = x.shape[0]
    tm = min(ROW_TILE, n)
    row = lambda i: (i, 0)
    return pl.pallas_call(
        _out_proj_kernel,
        grid=(n // tm,),
        in_specs=[pl.BlockSpec((tm, D_ATT), row), pl.BlockSpec((tm, D_CONV), row),
                  pl.BlockSpec((tm, D_MODEL), row), _resident(w_bf16.shape),
                  _resident(g.shape), _resident(b.shape)],
        out_specs=pl.BlockSpec((tm, D_MODEL), row),
        out_shape=jax.ShapeDtypeStruct((n, D_MODEL), F32),
        compiler_params=_params("parallel"),
        name="out_proj",
    )(att, conv, x, w_bf16, g, b)


ID_NONE = float(2 ** 24)


def _top_rows(s, ids, k):
    vals, sel = [], []
    for _ in range(k):
        m = jnp.max(s, axis=0, keepdims=True)
        i = jnp.min(jnp.where(s == m, ids, ID_NONE), axis=0, keepdims=True)
        vals.append(m)
        sel.append(i)
        s = jnp.where(ids == i, NEG_INF, s)
    return jnp.concatenate(vals, axis=0), jnp.concatenate(sel, axis=0)


def _pair_candidates(v1, v2, sub):
    assert PEER_TOPK == 2 * SUBLANES
    vals = [v1[0:1, :] + v2]
    ids = [sub, sub + SUBLANES]
    for i in range(1, SUBLANES):
        vals.append(v1[i:i + 1, :] + v2[0:SUBLANES, :])
        ids.append(sub + i * PEER_TOPK)
    vals.append(v1[SUBLANES:, :] + v2[0:1, :])
    ids.append((sub + SUBLANES) * PEER_TOPK)
    return jnp.concatenate(vals, axis=0), jnp.concatenate(ids, axis=0)


def _take_rows(table, idx):
    out = jnp.zeros(idx.shape, table.dtype)
    for i in range(table.shape[0]):
        out = jnp.where(idx == i, table[i:i + 1, :], out)
    return out


def _peer_route_kernel(h_ref, wq_ref, sk_ref, e_ref, g_ref):
    hb = h_ref[...].astype(BF16)
    tokens = hb.shape[0]
    half = PEER_KEY_DIM // 2
    key_ids = lax.broadcasted_iota(I32, (N_SUBKEYS, tokens), 0).astype(F32)
    sub = lax.broadcasted_iota(I32, (SUBLANES, tokens), 0).astype(F32)
    experts, gates = [], []
    for head in range(PEER_HEADS):
        top = []
        for p in range(2):
            c0 = head * PEER_KEY_DIM + p * half
            qhp = jnp.dot(hb, wq_ref[:, c0:c0 + half], preferred_element_type=F32)
            sc = _dot_t(sk_ref[p], qhp.astype(BF16))
            top.append(_top_rows(sc, key_ids, PEER_TOPK))
        (v1, i1), (v2, i2) = top
        cand, cand_ids = _pair_candidates(v1, v2, sub)
        c_top, c_id = _top_rows(cand, cand_ids, PEER_TOPK)
        rank1 = jnp.floor(c_id * (1.0 / PEER_TOPK))
        rank2 = c_id - rank1 * PEER_TOPK
        experts.append(_take_rows(i1, rank1) * N_SUBKEYS + _take_rows(i2, rank2))
        ex = jnp.exp(c_top - jnp.max(c_top, axis=0, keepdims=True))
        gates.append(ex / jnp.sum(ex, axis=0, keepdims=True))
    e_ref[...] = jnp.concatenate(experts, axis=0).T.astype(I32)
    g_ref[...] = jnp.concatenate(gates, axis=0).T


def _peer_route(h, wq_bf16, subkeys_bf16):
    n = h.shape[0]
    tm = min(ROW_TILE, n)
    row = lambda i: (i, 0)
    return pl.pallas_call(
        _peer_route_kernel,
        grid=(n // tm,),
        in_specs=[pl.BlockSpec((tm, D_MODEL), row), _resident(wq_bf16.shape),
                  _resident(subkeys_bf16.shape)],
        out_specs=[pl.BlockSpec((tm, PEER_SLOTS), row)] * 2,
        out_shape=[jax.ShapeDtypeStruct((n, PEER_SLOTS), I32),
                   jax.ShapeDtypeStruct((n, PEER_SLOTS), F32)],
        compiler_params=_params("parallel"),
        name="peer_route",
    )(h, wq_bf16, subkeys_bf16)


PEER_TOKENS = 64
FOLD = D_MODEL // LANES


def _peer_expert_kernel(e_ref, h_ref, g_ref, uv_hbm, o_ref, buf, sem):
    tokens = h_ref.shape[0]
    eye = (lax.broadcasted_iota(I32, (PEER_SLOTS, PEER_SLOTS), 0)
           == lax.broadcasted_iota(I32, (PEER_SLOTS, PEER_SLOTS), 1))

    def start(t, slot):
        for k in range(PEER_SLOTS):
            pltpu.make_async_copy(uv_hbm.at[e_ref[t, k]], buf.at[slot, k], sem.at[slot]).start()

    def wait(slot):
        pltpu.make_async_copy(uv_hbm.at[pl.ds(0, PEER_SLOTS)], buf.at[slot], sem.at[slot]).wait()

    start(0, 0)

    def token(t, carry):
        slot = t % 2

        @pl.when(t + 1 < tokens)
        def _():
            start(t + 1, 1 - slot)

        wait(slot)
        xrow = h_ref[pl.ds(t, 1), :]
        x = jnp.concatenate([xrow[:, j * LANES:(j + 1) * LANES] for j in range(FOLD)], axis=0)
        prod = buf[slot, :, 0:FOLD, :].astype(F32) * x[None]
        hid = jnp.sum(jnp.sum(prod, axis=1), axis=1, keepdims=True)
        gate = jnp.sum(jnp.where(eye, g_ref[pl.ds(t, 1), :], 0.0), axis=1, keepdims=True)
        a = gate * (0.5 * hid * (1.0 + lax.erf(hid * (2.0 ** -0.5))))
        a = jnp.broadcast_to(a, (PEER_SLOTS, LANES))[:, None, :]
        y = jnp.sum(a * buf[slot, :, FOLD:2 * FOLD, :].astype(F32), axis=0)
        o_ref[pl.ds(t, 1), :] = jnp.concatenate([y[j:j + 1, :] for j in range(FOLD)], axis=1)
        return carry

    lax.fori_loop(0, tokens, token, 0)


def _peer_expert(experts, h, gates, uv_tab):
    n = h.shape[0]
    tb = min(PEER_TOKENS, n)
    row = lambda i: (i, 0)
    return pl.pallas_call(
        _peer_expert_kernel,
        grid=(n // tb,),
        in_specs=[pl.BlockSpec((tb, PEER_SLOTS), row, memory_space=pltpu.SMEM),
                  pl.BlockSpec((tb, D_MODEL), row),
                  pl.BlockSpec((tb, PEER_SLOTS), row),
                  pl.BlockSpec(memory_space=pl.ANY)],
        out_specs=pl.BlockSpec((tb, D_MODEL), row),
        out_shape=jax.ShapeDtypeStruct((n, D_MODEL), F32),
        scratch_shapes=[pltpu.VMEM((2, PEER_SLOTS, 2 * FOLD, LANES), uv_tab.dtype),
                        pltpu.SemaphoreType.DMA((2,))],
        compiler_params=_params("arbitrary"),
        name="peer_expert",
    )(experts, h, gates, uv_tab)


def _tail_kernel(h_ref, y_ref, p_ref, g_ref, b_ref, wg_ref, wp_ref, o_ref):
    h2 = _layer_norm(DEEPNORM_ALPHA * h_ref[...] + y_ref[...], g_ref[...], b_ref[...])
    gate = jax.nn.sigmoid(jnp.dot(h2.astype(BF16), wg_ref[...], preferred_element_type=F32))
    emb = jnp.dot(p_ref[...].astype(BF16), wp_ref[...], preferred_element_type=F32)
    o_ref[...] = h2 + gate * emb


def _tail(h, y, p, g, b, wg_bf16, wp_bf16):
    n = h.shape[0]
    tm = min(ROW_TILE, n)
    row = lambda i: (i, 0)
    return pl.pallas_call(
        _tail_kernel,
        grid=(n // tm,),
        in_specs=[pl.BlockSpec((tm, D_MODEL), row), pl.BlockSpec((tm, D_MODEL), row),
                  pl.BlockSpec((tm, p.shape[1]), row), _resident(g.shape), _resident(b.shape),
                  _resident(wg_bf16.shape), _resident(wp_bf16.shape)],
        out_specs=pl.BlockSpec((tm, D_MODEL), row),
        out_shape=jax.ShapeDtypeStruct((n, D_MODEL), F32),
        compiler_params=_params("parallel"),
        name="tail",
    )(h, y, p, g, b, wg_bf16, wp_bf16)


def _rotary_tables(pos):
    half = HEAD_DIM // 2
    inv = ROPE_THETA ** (-jnp.arange(half, dtype=F32) / half)
    ang = pos.astype(F32)[:, None] * inv[None, :]
    cos, sin = jnp.cos(ang), jnp.sin(ang)
    return jnp.concatenate([cos, cos], axis=1), jnp.concatenate([-sin, sin], axis=1)


def _state_halo(state):
    return jnp.pad(state, ((0, 0), (SUBLANES - (CONV_WIDTH - 1), 0), (0, 0)))


def _layer_tail(x, att, conv, p, w):
    h1 = _out_proj(att, conv, x, w["w_out"], w["ln1_g"], w["ln1_b"])
    experts, gates = _peer_route(h1, w["peer_wq"], w["peer_subkeys"])
    y = _peer_expert(experts, h1, gates, w["peer_uv"])
    return _tail(h1, y, p, w["ln2_g"], w["ln2_b"], w["ple_gate"], w["ple_proj"])


def kernel(x_prompt, x_sample, state_attn_k, state_attn_v, state_conv, p_prompt, p_sample,
           w_in, w_conv, w_out, ln1_g, ln1_b, peer_wq, peer_subkeys, peer_u, peer_v,
           ln2_g, ln2_b, ple_gate, ple_proj):
    assert w_in.shape[0] == DEPTH
    bp, tp, _ = x_prompt.shape
    bs, ts, _ = x_sample.shape
    n_buf = state_attn_k.shape[2]
    w = {
        "w_in": w_in[0].astype(BF16), "w_out": w_out[0].astype(BF16),
        "ln1_g": ln1_g, "ln1_b": ln1_b, "ln2_g": ln2_g, "ln2_b": ln2_b,
        "peer_wq": peer_wq[0].astype(BF16), "peer_subkeys": peer_subkeys[0].astype(BF16),
        "peer_uv": jnp.concatenate([peer_u.reshape(-1, FOLD, LANES), peer_v.reshape(-1, FOLD, LANES)],
                                   axis=1).astype(BF16),
        "ple_gate": ple_gate[0].astype(BF16), "ple_proj": ple_proj[0].astype(BF16),
    }
    wc = w_conv[0]

    xp = x_prompt.reshape(bp * tp, D_MODEL)
    cos_p, sin_p = _rotary_tables(jnp.arange(tp))
    q, k, v, z, gb = _in_proj(xp, w["w_in"], cos_p, sin_p)
    conv = _conv(z, gb, jnp.zeros((bp, SUBLANES, D_CONV), F32), wc, bp, tp)
    att = _attn_prompt(q, k, v, bp, tp)
    y_prompt = _layer_tail(xp, att, conv, p_prompt[0].reshape(bp * tp, -1), w).reshape(bp, tp, D_MODEL)
    new_k_prompt = k.reshape(1, bp, tp, ATT_HEADS, HEAD_DIM)
    new_v_prompt = v.reshape(1, bp, tp, ATT_HEADS, HEAD_DIM)
    new_conv_prompt = z.reshape(bp, tp, D_CONV)[:, tp - (CONV_WIDTH - 1):][None]

    xs = x_sample.reshape(bs * ts, D_MODEL)
    cos_s, sin_s = _rotary_tables(jnp.tile(PAST_LEN + jnp.arange(ts), bs))
    q, k, v, z, gb = _in_proj(xs, w["w_in"], cos_s, sin_s)
    conv = _conv(z, gb, _state_halo(state_conv[0]), wc, bs, ts)
    att, new_k, new_v = _attn_sample(q, k, v, state_attn_k[0].reshape(bs, n_buf, D_ATT),
                                     state_attn_v[0].reshape(bs, n_buf, D_ATT), bs, ts)
    y_sample = _layer_tail(xs, att, conv, p_sample[0].reshape(bs * ts, -1), w).reshape(bs, ts, D_MODEL)
    new_k_sample = new_k.reshape(1, bs, n_buf, ATT_HEADS, HEAD_DIM)
    new_v_sample = new_v.reshape(1, bs, n_buf, ATT_HEADS, HEAD_DIM)
    z_ext = jnp.concatenate([state_conv[0], z.reshape(bs, ts, D_CONV)], axis=1)
    new_conv_sample = z_ext[:, -(CONV_WIDTH - 1):][None]

    return (y_prompt, y_sample, new_k_prompt, new_v_prompt, new_conv_prompt,
            new_k_sample, new_v_sample, new_conv_sample)
```

```python
import functools
import math

import jax
import jax.numpy as jnp
from jax import lax
from jax.experimental import pallas as pl
from jax.experimental.pallas import tpu as pltpu

F32 = jnp.float32
BF16 = jnp.bfloat16
I32 = jnp.int32

D_MODEL = 2048
PAST_LEN = 16384
ATT_HEADS = 8
HEAD_DIM = 128
D_ATT = ATT_HEADS * HEAD_DIM
D_CONV = D_MODEL - D_ATT
CONV_WIDTH = 3
DILATIONS = (1, 4, 16)
BAND = 128
ROPE_THETA = 10000.0
PEER_HEADS = 8
N_SUBKEYS = 128
PEER_KEY_DIM = 256
PEER_TOPK = 16
PEER_SLOTS = PEER_HEADS * PEER_TOPK
DEPTH = 1
DEEPNORM_ALPHA = (2.0 * DEPTH) ** 0.25
LN_EPS = 1e-5

LANES = 128
SUBLANES = 8
ROW_TILE = 256
VMEM_LIMIT = 56 * 1024 * 1024
NEG_INF = float("-inf")


def _params(*sem):
    return pltpu.CompilerParams(dimension_semantics=sem, vmem_limit_bytes=VMEM_LIMIT)


def _resident(shape):
    nd = len(shape)
    return pl.BlockSpec(shape, lambda *_: (0,) * nd, pipeline_mode=pl.Buffered(1))


def _layer_norm(y, g, b):
    mu = jnp.mean(y, axis=-1, keepdims=True)
    c = y - mu
    var = jnp.mean(c * c, axis=-1, keepdims=True)
    return c * lax.rsqrt(var + LN_EPS) * g + b


def _in_proj_kernel(x_ref, w_ref, cos_ref, sin_ref, q_ref, k_ref, v_ref, z_ref, gb_ref):
    xb = x_ref[...].astype(BF16)
    cos = cos_ref[...]
    sin = sin_ref[...]

    def proj(c):
        return jnp.dot(xb, w_ref[:, c * D_ATT:(c + 1) * D_ATT], preferred_element_type=F32)

    def store_rotary(a, o_ref):
        for h in range(ATT_HEADS):
            ah = a[:, h * HEAD_DIM:(h + 1) * HEAD_DIM]
            o_ref[:, h * HEAD_DIM:(h + 1) * HEAD_DIM] = (
                ah * cos + pltpu.roll(ah, HEAD_DIM // 2, axis=1) * sin)

    store_rotary(proj(0), q_ref)
    store_rotary(proj(1), k_ref)
    v_ref[...] = proj(2)
    u = proj(3)
    gb_ref[...] = proj(4)
    z_ref[...] = proj(5) * u


def _in_proj(x, w_bf16, cos_tab, sin_tab):
    n = x.shape[0]
    tm = min(ROW_TILE, n)
    n_tab = cos_tab.shape[0] // tm
    row = lambda i: (i, 0)
    out = jax.ShapeDtypeStruct((n, D_ATT), F32)
    return pl.pallas_call(
        _in_proj_kernel,
        grid=(n // tm,),
        in_specs=[pl.BlockSpec((tm, D_MODEL), row),
                  _resident(w_bf16.shape),
                  pl.BlockSpec((tm, HEAD_DIM), lambda i: (i % n_tab, 0)),
                  pl.BlockSpec((tm, HEAD_DIM), lambda i: (i % n_tab, 0))],
        out_specs=[pl.BlockSpec((tm, D_ATT), row)] * 5,
        out_shape=[out] * 5,
        compiler_params=_params("parallel"),
        name="in_proj",
    )(x, w_bf16, cos_tab, sin_tab)


def _conv_kernel(z_ref, zprev_ref, st_ref, gb_ref, wc_ref, o_ref):
    halo = jnp.where(pl.program_id(1) == 0, st_ref[...], zprev_ref[...])
    z = z_ref[...]
    row = lax.broadcasted_iota(I32, z.shape, 0)
    h1 = halo[SUBLANES - 1:SUBLANES, :]
    h2 = halo[SUBLANES - 2:SUBLANES - 1, :]
    zm1 = jnp.where(row == 0, h1, pltpu.roll(z, 1, axis=0))
    zm2 = jnp.where(row == 0, h2, jnp.where(row == 1, h1, pltpu.roll(z, 2, axis=0)))
    wc = wc_ref[...]
    o_ref[...] = gb_ref[...] * (wc[0:1, :] * zm2 + wc[1:2, :] * zm1 + wc[2:3, :] * z)


def _conv(z, gb, state_halo, w_conv, batch, seq):
    tt = min(ROW_TILE, seq)
    per_seq = seq // tt
    halo_per_tile = tt // SUBLANES
    z3 = z.reshape(batch, seq, D_CONV)
    gb3 = gb.reshape(batch, seq, D_CONV)
    tile = pl.BlockSpec((None, tt, D_CONV), lambda b, i: (b, i, 0))
    out = pl.pallas_call(
        _conv_kernel,
        grid=(batch, per_seq),
        in_specs=[tile,
                  pl.BlockSpec((None, SUBLANES, D_CONV),
                               lambda b, i: (b, jnp.maximum(i * halo_per_tile - 1, 0), 0)),
                  pl.BlockSpec((None, SUBLANES, D_CONV), lambda b, i: (b, 0, 0)),
                  tile,
                  pl.BlockSpec((CONV_WIDTH, D_CONV), lambda b, i: (0, 0))],
        out_specs=tile,
        out_shape=jax.ShapeDtypeStruct((batch, seq, D_CONV), F32),
        compiler_params=_params("parallel", "parallel"),
        name="conv",
    )(z3, z3, state_halo, gb3, w_conv)
    return out.reshape(batch * seq, D_CONV)


def _dot_t(a, b):
    return lax.dot_general(a, b, (((1,), (1,)), ((), ())), preferred_element_type=F32)


ATTN_UNROLL = 4


def _attn_prompt_kernel(q_ref, k_ref, v_ref, o_ref, acc_sc, m_sc, l_sc):
    seq = q_ref.shape[0]
    scale = HEAD_DIM ** -0.5
    qi = lax.broadcasted_iota(I32, (BAND, BAND), 0)
    kj = lax.broadcasted_iota(I32, (BAND, BAND), 1)
    cur_ok = kj <= qi
    prev_ok = kj >= qi

    for bi, d in enumerate(DILATIONS):
        nb = seq // (d * BAND)

        def block(blk, carry, bi=bi, d=d, nb=nb):
            r = blk // nb
            n = blk % nb
            start = r + n * (d * BAND)
            prev_start = jnp.maximum(start - d * BAND, 0)
            rows = pl.ds(start, BAND, stride=d) if d > 1 else pl.ds(pl.multiple_of(start, BAND), BAND)
            prows = pl.ds(prev_start, BAND, stride=d) if d > 1 else pl.ds(pl.multiple_of(prev_start, BAND), BAND)
            q = q_ref[rows, :].astype(BF16)
            s_cur = _dot_t(q, k_ref[rows, :].astype(BF16)) * scale
            s_prev = _dot_t(q, k_ref[prows, :].astype(BF16)) * scale
            s_cur = jnp.where(cur_ok, s_cur, NEG_INF)
            s_prev = jnp.where(prev_ok, s_prev, NEG_INF) + jnp.where(n > 0, 0.0, NEG_INF)
            m = jnp.maximum(jnp.max(s_cur, axis=1, keepdims=True),
                            jnp.max(s_prev, axis=1, keepdims=True))
            p_cur = jnp.exp(s_cur - m)
            p_prev = jnp.exp(s_prev - m)
            l = jnp.sum(p_cur, axis=1, keepdims=True) + jnp.sum(p_prev, axis=1, keepdims=True)
            acc = (jnp.dot(p_cur.astype(BF16), v_ref[rows, :].astype(BF16), preferred_element_type=F32)
                   + jnp.dot(p_prev.astype(BF16), v_ref[prows, :].astype(BF16), preferred_element_type=F32))
            acc_sc[bi, rows, :] = acc
            m_sc[bi, rows, :] = jnp.broadcast_to(m, (BAND, HEAD_DIM))
            l_sc[bi, rows, :] = jnp.broadcast_to(l, (BAND, HEAD_DIM))
            return carry

        lax.fori_loop(0, d * nb, block, 0, unroll=ATTN_UNROLL)

    big_m = jnp.maximum(jnp.maximum(m_sc[0], m_sc[1]), m_sc[2])
    num = jnp.zeros((seq, HEAD_DIM), F32)
    den = jnp.zeros((seq, HEAD_DIM), F32)
    for bi in range(len(DILATIONS)):
        w = jnp.exp(m_sc[bi] - big_m)
        num = num + w * acc_sc[bi]
        den = den + w * l_sc[bi]
    o_ref[...] = num / den


def _attn_prompt(q, k, v, batch, seq):
    q3, k3, v3 = (a.reshape(batch, seq, D_ATT) for a in (q, k, v))
    head = pl.BlockSpec((None, seq, HEAD_DIM), lambda b, h: (b, 0, h))
    nbr = len(DILATIONS)
    out = pl.pallas_call(
        _attn_prompt_kernel,
        grid=(batch, ATT_HEADS),
        in_specs=[head, head, head],
        out_specs=head,
        out_shape=jax.ShapeDtypeStruct((batch, seq, D_ATT), F32),
        scratch_shapes=[pltpu.VMEM((nbr, seq, HEAD_DIM), F32)] * 3,
        compiler_params=_params("parallel", "parallel"),
        name="attn_prompt",
    )(q3, k3, v3)
    return out.reshape(batch * seq, D_ATT)


def _branch_count(rel):
    cnt = jnp.zeros(rel.shape, F32)
    for d in DILATIONS:
        hit = (rel >= 0) & (rel <= d * BAND) & ((rel & (d - 1)) == 0)
        cnt = cnt + jnp.where(hit, 1.0, 0.0)
    return cnt


def _attn_sample_kernel(q_ref, kn_ref, vn_ref, ks_ref, vs_ref, o_ref, ko_ref, vo_ref):
    n_buf = ks_ref.shape[0]
    t_new = q_ref.shape[0]
    scale = HEAD_DIM ** -0.5
    q = q_ref[...].astype(BF16)
    ks = ks_ref[...]
    vs = vs_ref[...]
    kn = kn_ref[...]
    vn = vn_ref[...]
    s_st = _dot_t(q, ks.astype(BF16)) * scale
    s_nw = _dot_t(q, kn.astype(BF16)) * scale
    qpos = n_buf + lax.broadcasted_iota(I32, s_st.shape, 0)
    c_st = _branch_count(qpos - lax.broadcasted_iota(I32, s_st.shape, 1))
    c_nw = _branch_count(lax.broadcasted_iota(I32, s_nw.shape, 0)
                         - lax.broadcasted_iota(I32, s_nw.shape, 1))
    s_st = jnp.where(c_st > 0, s_st, NEG_INF)
    s_nw = jnp.where(c_nw > 0, s_nw, NEG_INF)
    m = jnp.maximum(jnp.max(s_st, axis=1, keepdims=True), jnp.max(s_nw, axis=1, keepdims=True))
    p_st = c_st * jnp.exp(s_st - m)
    p_nw = c_nw * jnp.exp(s_nw - m)
    l = jnp.sum(p_st, axis=1, keepdims=True) + jnp.sum(p_nw, axis=1, keepdims=True)
    acc = (jnp.dot(p_st.astype(BF16), vs.astype(BF16), preferred_element_type=F32)
           + jnp.dot(p_nw.astype(BF16), vn.astype(BF16), preferred_element_type=F32))
    o_ref[...] = acc / l
    keep = n_buf - t_new
    ko_ref[0:keep, :] = ks[t_new:, :]
    ko_ref[keep:, :] = kn
    vo_ref[0:keep, :] = vs[t_new:, :]
    vo_ref[keep:, :] = vn


def _attn_sample(q, k_new, v_new, k_state, v_state, batch, t_new):
    n_buf = k_state.shape[1]
    new = pl.BlockSpec((t_new, HEAD_DIM), lambda b, h: (b, h))
    buf = pl.BlockSpec((None, n_buf, HEAD_DIM), lambda b, h: (b, 0, h))
    buf_shape = jax.ShapeDtypeStruct((batch, n_buf, D_ATT), F32)
    return pl.pallas_call(
        _attn_sample_kernel,
        grid=(batch, ATT_HEADS),
        in_specs=[new, new, new, buf, buf],
        out_specs=[new, buf, buf],
        out_shape=[jax.ShapeDtypeStruct((batch * t_new, D_ATT), F32), buf_shape, buf_shape],
        compiler_params=_params("parallel", "parallel"),
        name="attn_sample",
    )(q, k_new, v_new, k_state, v_state)


def _out_proj_kernel(att_ref, conv_ref, x_ref, w_ref, g_ref, b_ref, o_ref):
    mixed = (jnp.dot(att_ref[...].astype(BF16), w_ref[0:D_ATT, :], preferred_element_type=F32)
             + jnp.dot(conv_ref[...].astype(BF16), w_ref[D_ATT:D_MODEL, :], preferred_element_type=F32))
    o_ref[...] = _layer_norm(DEEPNORM_ALPHA * x_ref[...] + mixed, g_ref[...], b_ref[...])


def _out_proj(att, conv, x, w_bf16, g, b):
    n = x.shape[0]
    tm = min(ROW_TILE, n)
    row = lambda i: (i, 0)
    return pl.pallas_call(
        _out_proj_kernel,
        grid=(n // tm,),
        in_specs=[pl.BlockSpec((tm, D_ATT), row), pl.BlockSpec((tm, D_CONV), row),
                  pl.BlockSpec((tm, D_MODEL), row), _resident(w_bf16.shape),
                  _resident(g.shape), _resident(b.shape)],
        out_specs=pl.BlockSpec((tm, D_MODEL), row),
        out_shape=jax.ShapeDtypeStruct((n, D_MODEL), F32),
        compiler_params=_params("parallel"),
        name="out_proj",
    )(att, conv, x, w_bf16, g, b)


ID_NONE = float(2 ** 24)


def _top_rows(s, ids, k):
    vals, sel = [], []
    for _ in range(k):
        m = jnp.max(s, axis=0, keepdims=True)
        i = jnp.min(jnp.where(s == m, ids, ID_NONE), axis=0, keepdims=True)
        vals.append(m)
        sel.append(i)
        s = jnp.where(ids == i, NEG_INF, s)
    return jnp.concatenate(vals, axis=0), jnp.concatenate(sel, axis=0)


def _pair_candidates(v1, v2, sub):
    assert PEER_TOPK == 2 * SUBLANES
    vals = [v1[0:1, :] + v2]
    ids = [sub, sub + SUBLANES]
    for i in range(1, SUBLANES):
        vals.append(v1[i:i + 1, :] + v2[0:SUBLANES, :])
        ids.append(sub + i * PEER_TOPK)
    vals.append(v1[SUBLANES:, :] + v2[0:1, :])
    ids.append((sub + SUBLANES) * PEER_TOPK)
    return jnp.concatenate(vals, axis=0), jnp.concatenate(ids, axis=0)


def _take_rows(table, idx):
    out = jnp.zeros(idx.shape, table.dtype)
    for i in range(table.shape[0]):
        out = jnp.where(idx == i, table[i:i + 1, :], out)
    return out


def _peer_route_kernel(h_ref, wq_ref, sk_ref, e_ref, g_ref):
    hb = h_ref[...].astype(BF16)
    tokens = hb.shape[0]
    half = PEER_KEY_DIM // 2
    key_ids = lax.broadcasted_iota(I32, (N_SUBKEYS, tokens), 0).astype(F32)
    sub = lax.broadcasted_iota(I32, (SUBLANES, tokens), 0).astype(F32)
    experts, gates = [], []
    for head in range(PEER_HEADS):
        top = []
        for p in range(2):
            c0 = head * PEER_KEY_DIM + p * half
            qhp = jnp.dot(hb, wq_ref[:, c0:c0 + half], preferred_element_type=F32)
            sc = _dot_t(sk_ref[p], qhp.astype(BF16))
            top.append(_top_rows(sc, key_ids, PEER_TOPK))
        (v1, i1), (v2, i2) = top
        cand, cand_ids = _pair_candidates(v1, v2, sub)
        c_top, c_id = _top_rows(cand, cand_ids, PEER_TOPK)
        rank1 = jnp.floor(c_id * (1.0 / PEER_TOPK))
        rank2 = c_id - rank1 * PEER_TOPK
        experts.append(_take_rows(i1, rank1) * N_SUBKEYS + _take_rows(i2, rank2))
        ex = jnp.exp(c_top - jnp.max(c_top, axis=0, keepdims=True))
        gates.append(ex / jnp.sum(ex, axis=0, keepdims=True))
    e_ref[...] = jnp.concatenate(experts, axis=0).T.astype(I32)
    g_ref[...] = jnp.concatenate(gates, axis=0).T


def _peer_route(h, wq_bf16, subkeys_bf16):
    n = h.shape[0]
    tm = min(ROW_TILE, n)
    row = lambda i: (i, 0)
    return pl.pallas_call(
        _peer_route_kernel,
        grid=(n // tm,),
        in_specs=[pl.BlockSpec((tm, D_MODEL), row), _resident(wq_bf16.shape),
                  _resident(subkeys_bf16.shape)],
        out_specs=[pl.BlockSpec((tm, PEER_SLOTS), row)] * 2,
        out_shape=[jax.ShapeDtypeStruct((n, PEER_SLOTS), I32),
                   jax.ShapeDtypeStruct((n, PEER_SLOTS), F32)],
        compiler_params=_params("parallel"),
        name="peer_route",
    )(h, wq_bf16, subkeys_bf16)


PEER_TOKENS = 64
PEER_RING = 4
FOLD = D_MODEL // LANES


def _peer_expert_kernel(e_ref, e_next_ref, h_ref, g_ref, uv_hbm, o_ref, buf, sem, *, steps):
    tokens = h_ref.shape[0]
    ahead = PEER_RING - 1
    assert tokens % PEER_RING == 0 and tokens > ahead
    step = pl.program_id(0)
    eye = (lax.broadcasted_iota(I32, (PEER_SLOTS, PEER_SLOTS), 0)
           == lax.broadcasted_iota(I32, (PEER_SLOTS, PEER_SLOTS), 1))

    def start(ids_ref, t, slot):
        for k in range(PEER_SLOTS):
            pltpu.make_async_copy(uv_hbm.at[ids_ref[t, k]], buf.at[slot, k], sem.at[slot]).start()

    def wait(slot):
        pltpu.make_async_copy(uv_hbm.at[pl.ds(0, PEER_SLOTS)], buf.at[slot], sem.at[slot]).wait()

    def compute(t, slot):
        xrow = h_ref[pl.ds(t, 1), :]
        x = jnp.concatenate([xrow[:, j * LANES:(j + 1) * LANES] for j in range(FOLD)], axis=0)
        prod = buf[slot, :, 0:FOLD, :].astype(F32) * x[None]
        hid = jnp.sum(jnp.sum(prod, axis=1), axis=1, keepdims=True)
        gate = jnp.sum(jnp.where(eye, g_ref[pl.ds(t, 1), :], 0.0), axis=1, keepdims=True)
        a = gate * (0.5 * hid * (1.0 + lax.erf(hid * (2.0 ** -0.5))))
        a = jnp.broadcast_to(a, (PEER_SLOTS, LANES))[:, None, :]
        y = jnp.sum(a * buf[slot, :, FOLD:2 * FOLD, :].astype(F32), axis=0)
        o_ref[pl.ds(t, 1), :] = jnp.concatenate([y[j:j + 1, :] for j in range(FOLD)], axis=1)

    @pl.when(step == 0)
    def _():
        for t in range(ahead):
            start(e_ref, t, t)

    def token(t, carry):
        start(e_ref, t + ahead, (t + ahead) % PEER_RING)
        wait(t % PEER_RING)
        compute(t, t % PEER_RING)
        return carry

    lax.fori_loop(0, tokens - ahead, token, 0)
    for j in range(ahead):
        t = tokens - ahead + j
        start(e_next_ref, j, j % PEER_RING)
        wait(t % PEER_RING)
        compute(t, t % PEER_RING)

    @pl.when(step == steps - 1)
    def _():
        for j in range(ahead):
            wait(j % PEER_RING)


def _peer_expert(experts, h, gates, uv_tab):
    n = h.shape[0]
    tb = min(PEER_TOKENS, n)
    steps = n // tb
    row = lambda i: (i, 0)
    return pl.pallas_call(
        functools.partial(_peer_expert_kernel, steps=steps),
        grid=(steps,),
        in_specs=[pl.BlockSpec((tb, PEER_SLOTS), row, memory_space=pltpu.SMEM),
                  pl.BlockSpec((tb, PEER_SLOTS), lambda i: (jnp.minimum(i + 1, steps - 1), 0),
                               memory_space=pltpu.SMEM),
                  pl.BlockSpec((tb, D_MODEL), row),
                  pl.BlockSpec((tb, PEER_SLOTS), row),
                  pl.BlockSpec(memory_space=pl.ANY)],
        out_specs=pl.BlockSpec((tb, D_MODEL), row),
        out_shape=jax.ShapeDtypeStruct((n, D_MODEL), F32),
        scratch_shapes=[pltpu.VMEM((PEER_RING, PEER_SLOTS, 2 * FOLD, LANES), uv_tab.dtype),
                        pltpu.SemaphoreType.DMA((PEER_RING,))],
        compiler_params=_params("arbitrary"),
        name="peer_expert",
    )(experts, experts, h, gates, uv_tab)


def _tail_kernel(h_ref, y_ref, p_ref, g_ref, b_ref, wg_ref, wp_ref, o_ref):
    h2 = _layer_norm(DEEPNORM_ALPHA * h_ref[...] + y_ref[...], g_ref[...], b_ref[...])
    gate = jax.nn.sigmoid(jnp.dot(h2.astype(BF16), wg_ref[...], preferred_element_type=F32))
    emb = jnp.dot(p_ref[...].astype(BF16), wp_ref[...], preferred_element_type=F32)
    o_ref[...] = h2 + gate * emb


def _tail(h, y, p, g, b, wg_bf16, wp_bf16):
    n = h.shape[0]
    tm = min(ROW_TILE, n)
    row = lambda i: (i, 0)
    return pl.pallas_call(
        _tail_kernel,
        grid=(n // tm,),
        in_specs=[pl.BlockSpec((tm, D_MODEL), row), pl.BlockSpec((tm, D_MODEL), row),
                  pl.BlockSpec((tm, p.shape[1]), row), _resident(g.shape), _resident(b.shape),
                  _resident(wg_bf16.shape), _resident(wp_bf16.shape)],
        out_specs=pl.BlockSpec((tm, D_MODEL), row),
        out_shape=jax.ShapeDtypeStruct((n, D_MODEL), F32),
        compiler_params=_params("parallel"),
        name="tail",
    )(h, y, p, g, b, wg_bf16, wp_bf16)


def _rotary_tables(pos):
    half = HEAD_DIM // 2
    inv = ROPE_THETA ** (-jnp.arange(half, dtype=F32) / half)
    ang = pos.astype(F32)[:, None] * inv[None, :]
    cos, sin = jnp.cos(ang), jnp.sin(ang)
    return jnp.concatenate([cos, cos], axis=1), jnp.concatenate([-sin, sin], axis=1)


def _state_halo(state):
    return jnp.pad(state, ((0, 0), (SUBLANES - (CONV_WIDTH - 1), 0), (0, 0)))


def _layer_tail(x, att, conv, p, w):
    h1 = _out_proj(att, conv, x, w["w_out"], w["ln1_g"], w["ln1_b"])
    experts, gates = _peer_route(h1, w["peer_wq"], w["peer_subkeys"])
    y = _peer_expert(experts, h1, gates, w["peer_uv"])
    return _tail(h1, y, p, w["ln2_g"], w["ln2_b"], w["ple_gate"], w["ple_proj"])


def kernel(x_prompt, x_sample, state_attn_k, state_attn_v, state_conv, p_prompt, p_sample,
           w_in, w_conv, w_out, ln1_g, ln1_b, peer_wq, peer_subkeys, peer_u, peer_v,
           ln2_g, ln2_b, ple_gate, ple_proj):
    assert w_in.shape[0] == DEPTH
    bp, tp, _ = x_prompt.shape
    bs, ts, _ = x_sample.shape
    n_buf = state_attn_k.shape[2]
    w = {
        "w_in": w_in[0].astype(BF16), "w_out": w_out[0].astype(BF16),
        "ln1_g": ln1_g, "ln1_b": ln1_b, "ln2_g": ln2_g, "ln2_b": ln2_b,
        "peer_wq": peer_wq[0].astype(BF16), "peer_subkeys": peer_subkeys[0].astype(BF16),
        "peer_uv": jnp.concatenate([peer_u.reshape(-1, FOLD, LANES), peer_v.reshape(-1, FOLD, LANES)],
                                   axis=1).astype(BF16),
        "ple_gate": ple_gate[0].astype(BF16), "ple_proj": ple_proj[0].astype(BF16),
    }
    wc = w_conv[0]

    xp = x_prompt.reshape(bp * tp, D_MODEL)
    cos_p, sin_p = _rotary_tables(jnp.arange(tp))
    q, k, v, z, gb = _in_proj(xp, w["w_in"], cos_p, sin_p)
    conv = _conv(z, gb, jnp.zeros((bp, SUBLANES, D_CONV), F32), wc, bp, tp)
    att = _attn_prompt(q, k, v, bp, tp)
    y_prompt = _layer_tail(xp, att, conv, p_prompt[0].reshape(bp * tp, -1), w).reshape(bp, tp, D_MODEL)
    new_k_prompt = k.reshape(1, bp, tp, ATT_HEADS, HEAD_DIM)
    new_v_prompt = v.reshape(1, bp, tp, ATT_HEADS, HEAD_DIM)
    new_conv_prompt = z.reshape(bp, tp, D_CONV)[:, tp - (CONV_WIDTH - 1):][None]

    xs = x_sample.reshape(bs * ts, D_MODEL)
    cos_s, sin_s = _rotary_tables(jnp.tile(PAST_LEN + jnp.arange(ts), bs))
    q, k, v, z, gb = _in_proj(xs, w["w_in"], cos_s, sin_s)
    conv = _conv(z, gb, _state_halo(state_conv[0]), wc, bs, ts)
    att, new_k, new_v = _attn_sample(q, k, v, state_attn_k[0].reshape(bs, n_buf, D_ATT),
                                     state_attn_v[0].reshape(bs, n_buf, D_ATT), bs, ts)
    y_sample = _layer_tail(xs, att, conv, p_sample[0].reshape(bs * ts, -1), w).reshape(bs, ts, D_MODEL)
    new_k_sample = new_k.reshape(1, bs, n_buf, ATT_HEADS, HEAD_DIM)
    new_v_sample = new_v.reshape(1, bs, n_buf, ATT_HEADS, HEAD_DIM)
    z_ext = jnp.concatenate([state_conv[0], z.reshape(bs, ts, D_CONV)], axis=1)
    new_conv_sample = z_ext[:, -(CONV_WIDTH - 1):][None]

    return (y_prompt, y_sample, new_k_prompt, new_v_prompt, new_conv_prompt,
            new_k_sample, new_v_sample, new_conv_sample)
```

```python
import functools
import math

import jax
import jax.numpy as jnp
from jax import lax
from jax.experimental import pallas as pl
from jax.experimental.pallas import tpu as pltpu

F32 = jnp.float32
BF16 = jnp.bfloat16
I32 = jnp.int32

D_MODEL = 2048
PAST_LEN = 16384
ATT_HEADS = 8
HEAD_DIM = 128
D_ATT = ATT_HEADS * HEAD_DIM
D_CONV = D_MODEL - D_ATT
CONV_WIDTH = 3
DILATIONS = (1, 4, 16)
BAND = 128
ROPE_THETA = 10000.0
PEER_HEADS = 8
N_SUBKEYS = 128
PEER_KEY_DIM = 256
PEER_TOPK = 16
PEER_SLOTS = PEER_HEADS * PEER_TOPK
DEPTH = 1
DEEPNORM_ALPHA = (2.0 * DEPTH) ** 0.25
LN_EPS = 1e-5

LANES = 128
SUBLANES = 8
FOLD = D_MODEL // LANES
ROW_TILE = 256
VMEM_LIMIT = 56 * 1024 * 1024
NEG_INF = float("-inf")


def _params(*sem):
    return pltpu.CompilerParams(dimension_semantics=sem, vmem_limit_bytes=VMEM_LIMIT)


def _resident(shape):
    nd = len(shape)
    return pl.BlockSpec(shape, lambda *_: (0,) * nd, pipeline_mode=pl.Buffered(1))


def _layer_norm(y, g, b):
    mu = jnp.mean(y, axis=-1, keepdims=True)
    c = y - mu
    var = jnp.mean(c * c, axis=-1, keepdims=True)
    return c * lax.rsqrt(var + LN_EPS) * g + b


def _in_proj_kernel(x_ref, w_ref, cos_ref, sin_ref, q_ref, k_ref, v_ref, z_ref, gb_ref):
    xb = x_ref[...].astype(BF16)
    cos = cos_ref[...]
    sin = sin_ref[...]

    def proj(c):
        return jnp.dot(xb, w_ref[:, c * D_ATT:(c + 1) * D_ATT], preferred_element_type=F32)

    def store_rotary(a, o_ref):
        for h in range(ATT_HEADS):
            ah = a[:, h * HEAD_DIM:(h + 1) * HEAD_DIM]
            o_ref[:, h * HEAD_DIM:(h + 1) * HEAD_DIM] = (
                ah * cos + pltpu.roll(ah, HEAD_DIM // 2, axis=1) * sin)

    store_rotary(proj(0), q_ref)
    store_rotary(proj(1), k_ref)
    v_ref[...] = proj(2)
    u = proj(3)
    gb_ref[...] = proj(4)
    z_ref[...] = proj(5) * u


def _in_proj(x, w_bf16, cos_tab, sin_tab):
    n = x.shape[0]
    tm = min(ROW_TILE, n)
    n_tab = cos_tab.shape[0] // tm
    row = lambda i: (i, 0)
    out = jax.ShapeDtypeStruct((n, D_ATT), F32)
    return pl.pallas_call(
        _in_proj_kernel,
        grid=(n // tm,),
        in_specs=[pl.BlockSpec((tm, D_MODEL), row),
                  _resident(w_bf16.shape),
                  pl.BlockSpec((tm, HEAD_DIM), lambda i: (i % n_tab, 0)),
                  pl.BlockSpec((tm, HEAD_DIM), lambda i: (i % n_tab, 0))],
        out_specs=[pl.BlockSpec((tm, D_ATT), row)] * 5,
        out_shape=[out] * 5,
        compiler_params=_params("parallel"),
        name="in_proj",
    )(x, w_bf16, cos_tab, sin_tab)


def _conv_kernel(z_ref, zprev_ref, st_ref, gb_ref, wc_ref, o_ref):
    halo = jnp.where(pl.program_id(1) == 0, st_ref[...], zprev_ref[...])
    z = z_ref[...]
    row = lax.broadcasted_iota(I32, z.shape, 0)
    h1 = halo[SUBLANES - 1:SUBLANES, :]
    h2 = halo[SUBLANES - 2:SUBLANES - 1, :]
    zm1 = jnp.where(row == 0, h1, pltpu.roll(z, 1, axis=0))
    zm2 = jnp.where(row == 0, h2, jnp.where(row == 1, h1, pltpu.roll(z, 2, axis=0)))
    wc = wc_ref[...]
    o_ref[...] = gb_ref[...] * (wc[0:1, :] * zm2 + wc[1:2, :] * zm1 + wc[2:3, :] * z)


def _conv(z, gb, state_halo, w_conv, batch, seq):
    tt = min(ROW_TILE, seq)
    per_seq = seq // tt
    halo_per_tile = tt // SUBLANES
    z3 = z.reshape(batch, seq, D_CONV)
    gb3 = gb.reshape(batch, seq, D_CONV)
    tile = pl.BlockSpec((None, tt, D_CONV), lambda b, i: (b, i, 0))
    out = pl.pallas_call(
        _conv_kernel,
        grid=(batch, per_seq),
        in_specs=[tile,
                  pl.BlockSpec((None, SUBLANES, D_CONV),
                               lambda b, i: (b, jnp.maximum(i * halo_per_tile - 1, 0), 0)),
                  pl.BlockSpec((None, SUBLANES, D_CONV), lambda b, i: (b, 0, 0)),
                  tile,
                  pl.BlockSpec((CONV_WIDTH, D_CONV), lambda b, i: (0, 0))],
        out_specs=tile,
        out_shape=jax.ShapeDtypeStruct((batch, seq, D_CONV), F32),
        compiler_params=_params("parallel", "parallel"),
        name="conv",
    )(z3, z3, state_halo, gb3, w_conv)
    return out.reshape(batch * seq, D_CONV)


def _dot_t(a, b):
    return lax.dot_general(a, b, (((1,), (1,)), ((), ())), preferred_element_type=F32)


ATTN_UNROLL = 4


def _attn_prompt_kernel(q_ref, k_ref, v_ref, o_ref, acc_sc, m_sc, l_sc):
    seq = q_ref.shape[0]
    scale = HEAD_DIM ** -0.5
    qi = lax.broadcasted_iota(I32, (BAND, BAND), 0)
    kj = lax.broadcasted_iota(I32, (BAND, BAND), 1)
    cur_ok = kj <= qi
    prev_ok = kj >= qi

    for bi, d in enumerate(DILATIONS):
        nb = seq // (d * BAND)

        def block(blk, carry, bi=bi, d=d, nb=nb):
            r = blk // nb
            n = blk % nb
            start = r + n * (d * BAND)
            prev_start = jnp.maximum(start - d * BAND, 0)
            rows = pl.ds(start, BAND, stride=d) if d > 1 else pl.ds(pl.multiple_of(start, BAND), BAND)
            prows = pl.ds(prev_start, BAND, stride=d) if d > 1 else pl.ds(pl.multiple_of(prev_start, BAND), BAND)
            q = q_ref[rows, :].astype(BF16)
            s_cur = _dot_t(q, k_ref[rows, :].astype(BF16)) * scale
            s_prev = _dot_t(q, k_ref[prows, :].astype(BF16)) * scale
            s_cur = jnp.where(cur_ok, s_cur, NEG_INF)
            s_prev = jnp.where(prev_ok, s_prev, NEG_INF) + jnp.where(n > 0, 0.0, NEG_INF)
            m = jnp.maximum(jnp.max(s_cur, axis=1, keepdims=True),
                            jnp.max(s_prev, axis=1, keepdims=True))
            p_cur = jnp.exp(s_cur - m)
            p_prev = jnp.exp(s_prev - m)
            l = jnp.sum(p_cur, axis=1, keepdims=True) + jnp.sum(p_prev, axis=1, keepdims=True)
            acc = (jnp.dot(p_cur.astype(BF16), v_ref[rows, :].astype(BF16), preferred_element_type=F32)
                   + jnp.dot(p_prev.astype(BF16), v_ref[prows, :].astype(BF16), preferred_element_type=F32))
            acc_sc[bi, rows, :] = acc
            m_sc[bi, rows, :] = jnp.broadcast_to(m, (BAND, HEAD_DIM))
            l_sc[bi, rows, :] = jnp.broadcast_to(l, (BAND, HEAD_DIM))
            return carry

        lax.fori_loop(0, d * nb, block, 0, unroll=ATTN_UNROLL)

    big_m = jnp.maximum(jnp.maximum(m_sc[0], m_sc[1]), m_sc[2])
    num = jnp.zeros((seq, HEAD_DIM), F32)
    den = jnp.zeros((seq, HEAD_DIM), F32)
    for bi in range(len(DILATIONS)):
        w = jnp.exp(m_sc[bi] - big_m)
        num = num + w * acc_sc[bi]
        den = den + w * l_sc[bi]
    o_ref[...] = num / den


def _attn_prompt(q, k, v, batch, seq):
    q3, k3, v3 = (a.reshape(batch, seq, D_ATT) for a in (q, k, v))
    head = pl.BlockSpec((None, seq, HEAD_DIM), lambda b, h: (b, 0, h))
    nbr = len(DILATIONS)
    out = pl.pallas_call(
        _attn_prompt_kernel,
        grid=(batch, ATT_HEADS),
        in_specs=[head, head, head],
        out_specs=head,
        out_shape=jax.ShapeDtypeStruct((batch, seq, D_ATT), F32),
        scratch_shapes=[pltpu.VMEM((nbr, seq, HEAD_DIM), F32)] * 3,
        compiler_params=_params("parallel", "parallel"),
        name="attn_prompt",
    )(q3, k3, v3)
    return out.reshape(batch * seq, D_ATT)


def _branch_count(rel):
    cnt = jnp.zeros(rel.shape, F32)
    for d in DILATIONS:
        hit = (rel >= 0) & (rel <= d * BAND) & ((rel & (d - 1)) == 0)
        cnt = cnt + jnp.where(hit, 1.0, 0.0)
    return cnt


SAMPLE_CHUNK = 512
M_FLOOR = -1e30


def _attn_sample_kernel(q_ref, kn_ref, vn_ref, kn3_ref, vn3_ref, ks_ref, ks_next_ref, vs_ref, vs_next_ref,
                        o_ref, ko_ref, vo_ref, m_sc, l_sc, acc_sc, *, n_buf, chunks):
    c = pl.program_id(1)
    rows = ks_ref.shape[0]
    t_new = q_ref.shape[0]
    scale = HEAD_DIM ** -0.5
    log_heads = int(math.log2(ATT_HEADS))
    log_new = int(math.log2(t_new))
    assert ATT_HEADS == 1 << log_heads and t_new == 1 << log_new

    def by_head(ref):
        return jnp.concatenate([ref[:, h * HEAD_DIM:(h + 1) * HEAD_DIM] for h in range(ATT_HEADS)], axis=0)

    @pl.when(c == 0)
    def _():
        m_sc[...] = jnp.full(m_sc.shape, M_FLOOR, F32)
        l_sc[...] = jnp.zeros(l_sc.shape, F32)
        acc_sc[...] = jnp.zeros(acc_sc.shape, F32)

    q = by_head(q_ref).astype(BF16)

    def accumulate(k2, v2, key_head, rel):
        q_head = lax.shift_right_logical(lax.broadcasted_iota(I32, rel.shape, 0), log_new)
        cnt = jnp.where(key_head == q_head, _branch_count(rel), 0.0)
        s = jnp.where(cnt > 0, _dot_t(q, k2.astype(BF16)) * scale, NEG_INF)
        m_old = m_sc[...]
        m_new = jnp.maximum(m_old, jnp.max(s, axis=1, keepdims=True))
        alpha = jnp.exp(m_old - m_new)
        p = cnt * jnp.exp(s - m_new)
        l_sc[...] = alpha * l_sc[...] + jnp.sum(p, axis=1, keepdims=True)
        acc_sc[...] = alpha * acc_sc[...] + jnp.dot(p.astype(BF16), v2.astype(BF16),
                                                    preferred_element_type=F32)
        m_sc[...] = m_new

    ks = ks_ref[...]
    vs = vs_ref[...]
    n = rows * ATT_HEADS
    col = lax.broadcasted_iota(I32, (ATT_HEADS * t_new, n), 1)
    qrow = lax.broadcasted_iota(I32, (ATT_HEADS * t_new, n), 0)
    kpos = c * rows + lax.shift_right_logical(col, log_heads)
    qpos = n_buf + (qrow & (t_new - 1))
    accumulate(ks.reshape(n, HEAD_DIM), vs.reshape(n, HEAD_DIM), col & (ATT_HEADS - 1), qpos - kpos)

    last = c == chunks - 1
    ko_ref[0:rows - t_new] = ks[t_new:]
    vo_ref[0:rows - t_new] = vs[t_new:]
    ko_ref[rows - t_new:rows] = jnp.where(last, kn3_ref[...], ks_next_ref[...])
    vo_ref[rows - t_new:rows] = jnp.where(last, vn3_ref[...], vs_next_ref[...])

    @pl.when(last)
    def _():
        m2 = ATT_HEADS * t_new
        ncol = lax.broadcasted_iota(I32, (m2, m2), 1)
        nrow = lax.broadcasted_iota(I32, (m2, m2), 0)
        accumulate(by_head(kn_ref), by_head(vn_ref), lax.shift_right_logical(ncol, log_new),
                   (nrow & (t_new - 1)) - (ncol & (t_new - 1)))
        out = acc_sc[...] / l_sc[...]
        for h in range(ATT_HEADS):
            o_ref[:, h * HEAD_DIM:(h + 1) * HEAD_DIM] = out[h * t_new:(h + 1) * t_new, :]


def _attn_sample(q, k_new, v_new, k_state, v_state, batch, t_new):
    n_buf = k_state.shape[2]
    rows = min(SAMPLE_CHUNK, n_buf)
    chunks = n_buf // rows
    assert t_new == SUBLANES and rows % t_new == 0
    kn3 = k_new.reshape(batch * t_new, ATT_HEADS, HEAD_DIM)
    vn3 = v_new.reshape(batch * t_new, ATT_HEADS, HEAD_DIM)
    new = pl.BlockSpec((t_new, D_ATT), lambda b, c: (b, 0))
    new3 = pl.BlockSpec((t_new, ATT_HEADS, HEAD_DIM), lambda b, c: (b, 0, 0))
    buf = pl.BlockSpec((None, None, rows, ATT_HEADS, HEAD_DIM), lambda b, c: (0, b, c, 0, 0))
    per = rows // t_new
    nxt = pl.BlockSpec((None, None, t_new, ATT_HEADS, HEAD_DIM),
                       lambda b, c: (0, b, jnp.minimum((c + 1) * per, n_buf // t_new - 1), 0, 0))
    buf_shape = jax.ShapeDtypeStruct(k_state.shape, F32)
    qrows = ATT_HEADS * t_new
    return pl.pallas_call(
        functools.partial(_attn_sample_kernel, n_buf=n_buf, chunks=chunks),
        grid=(batch, chunks),
        in_specs=[new, new, new, new3, new3, buf, nxt, buf, nxt],
        out_specs=[new, buf, buf],
        out_shape=[jax.ShapeDtypeStruct((batch * t_new, D_ATT), F32), buf_shape, buf_shape],
        scratch_shapes=[pltpu.VMEM((qrows, 1), F32), pltpu.VMEM((qrows, 1), F32),
                        pltpu.VMEM((qrows, HEAD_DIM), F32)],
        compiler_params=_params("parallel", "arbitrary"),
        name="attn_sample",
    )(q, k_new, v_new, kn3, vn3, k_state, k_state, v_state, v_state)


def _out_proj_kernel(att_ref, conv_ref, x_ref, w_ref, g_ref, b_ref, o_ref):
    mixed = (jnp.dot(att_ref[...].astype(BF16), w_ref[0:D_ATT, :], preferred_element_type=F32)
             + jnp.dot(conv_ref[...].astype(BF16), w_ref[D_ATT:D_MODEL, :], preferred_element_type=F32))
    o_ref[...] = _layer_norm(DEEPNORM_ALPHA * x_ref[...] + mixed, g_ref[...], b_ref[...])


def _out_proj(att, conv, x, w_bf16, g, b):
    n = x.shape[0]
    tm = min(ROW_TILE, n)
    row = lambda i: (i, 0)
    return pl.pallas_call(
        _out_proj_kernel,
        grid=(n // tm,),
        in_specs=[pl.BlockSpec((tm, D_ATT), row), pl.BlockSpec((tm, D_CONV), row),
                  pl.BlockSpec((tm, D_MODEL), row), _resident(w_bf16.shape),
                  _resident(g.shape), _resident(b.shape)],
        out_specs=pl.BlockSpec((tm, D_MODEL), row),
        out_shape=jax.ShapeDtypeStruct((n, D_MODEL), F32),
        compiler_params=_params("parallel"),
        name="out_proj",
    )(att, conv, x, w_bf16, g, b)


ID_NONE = float(2 ** 24)


def _top_rows(s, ids, k):
    vals, sel = [], []
    for _ in range(k):
        m = jnp.max(s, axis=0, keepdims=True)
        i = jnp.min(jnp.where(s == m, ids, ID_NONE), axis=0, keepdims=True)
        vals.append(m)
        sel.append(i)
        s = jnp.where(ids == i, NEG_INF, s)
    return jnp.concatenate(vals, axis=0), jnp.concatenate(sel, axis=0)


def _pair_candidates(v1, v2, sub):
    assert PEER_TOPK == 2 * SUBLANES
    vals = [v1[0:1, :] + v2]
    ids = [sub, sub + SUBLANES]
    for i in range(1, SUBLANES):
        vals.append(v1[i:i + 1, :] + v2[0:SUBLANES, :])
        ids.append(sub + i * PEER_TOPK)
    vals.append(v1[SUBLANES:, :] + v2[0:1, :])
    ids.append((sub + SUBLANES) * PEER_TOPK)
    return jnp.concatenate(vals, axis=0), jnp.concatenate(ids, axis=0)


def _take_rows(table, idx):
    out = jnp.zeros(idx.shape, table.dtype)
    for i in range(table.shape[0]):
        out = jnp.where(idx == i, table[i:i + 1, :], out)
    return out


def _peer_route_kernel(h_ref, wq_ref, sk_ref, e_ref, g_ref):
    hb = h_ref[...].astype(BF16)
    tokens = hb.shape[0]
    half = PEER_KEY_DIM // 2
    key_ids = lax.broadcasted_iota(I32, (N_SUBKEYS, tokens), 0).astype(F32)
    sub = lax.broadcasted_iota(I32, (SUBLANES, tokens), 0).astype(F32)
    experts, gates = [], []
    for head in range(PEER_HEADS):
        top = []
        for p in range(2):
            c0 = head * PEER_KEY_DIM + p * half
            qhp = jnp.dot(hb, wq_ref[:, c0:c0 + half], preferred_element_type=F32)
            sc = _dot_t(sk_ref[p], qhp.astype(BF16))
            top.append(_top_rows(sc, key_ids, PEER_TOPK))
        (v1, i1), (v2, i2) = top
        cand, cand_ids = _pair_candidates(v1, v2, sub)
        c_top, c_id = _top_rows(cand, cand_ids, PEER_TOPK)
        rank1 = jnp.floor(c_id * (1.0 / PEER_TOPK))
        rank2 = c_id - rank1 * PEER_TOPK
        experts.append(_take_rows(i1, rank1) * N_SUBKEYS + _take_rows(i2, rank2))
        ex = jnp.exp(c_top - jnp.max(c_top, axis=0, keepdims=True))
        gates.append(ex / jnp.sum(ex, axis=0, keepdims=True))
    e_ref[...] = jnp.concatenate(experts, axis=0).T.astype(I32)
    g_ref[...] = jnp.concatenate([jnp.broadcast_to(g[k:k + 1, :], (FOLD, tokens))
                                  for g in gates for k in range(PEER_TOPK)], axis=0).T


def _peer_route(h, wq_bf16, subkeys_bf16):
    n = h.shape[0]
    tm = min(ROW_TILE, n)
    row = lambda i: (i, 0)
    return pl.pallas_call(
        _peer_route_kernel,
        grid=(n // tm,),
        in_specs=[pl.BlockSpec((tm, D_MODEL), row), _resident(wq_bf16.shape),
                  _resident(subkeys_bf16.shape)],
        out_specs=[pl.BlockSpec((tm, PEER_SLOTS), row), pl.BlockSpec((tm, PEER_SLOTS * FOLD), row)],
        out_shape=[jax.ShapeDtypeStruct((n, PEER_SLOTS), I32),
                   jax.ShapeDtypeStruct((n, PEER_SLOTS * FOLD), F32)],
        compiler_params=_params("parallel"),
        name="peer_route",
    )(h, wq_bf16, subkeys_bf16)


PEER_TOKENS = 64
PEER_GROUP = 4
PEER_RING = 2 * PEER_GROUP


def _fold(row):
    return jnp.concatenate([row[:, j * LANES:(j + 1) * LANES] for j in range(FOLD)], axis=0)


def _unfold(folded):
    return jnp.concatenate([folded[j:j + 1, :] for j in range(FOLD)], axis=1)


def _peer_expert_kernel(e_ref, e_next_ref, h_ref, g_ref, uv_hbm, o_ref, buf, sem, *, steps):
    tokens = h_ref.shape[0]
    groups = tokens // PEER_GROUP
    assert tokens % PEER_RING == 0 and FOLD <= LANES and LANES % FOLD == 0
    step = pl.program_id(0)
    rows = PEER_SLOTS * FOLD
    own = (lax.broadcasted_iota(I32, (FOLD, rows), 1) & (FOLD - 1)) == lax.broadcasted_iota(I32, (FOLD, rows), 0)
    pos = lax.broadcasted_iota(I32, (FOLD, LANES), 1) & (FOLD - 1)
    stages = 4
    per_burst = PEER_SLOTS // stages

    def start_burst(ids_ref, t, slot, burst):
        for k in range(burst * per_burst, (burst + 1) * per_burst):
            pltpu.make_async_copy(uv_hbm.at[ids_ref[t, k]], buf.at[slot, k], sem.at[slot]).start(priority=k % 2)

    def wait(slot):
        pltpu.make_async_copy(uv_hbm.at[pl.ds(0, PEER_SLOTS)], buf.at[slot], sem.at[slot]).wait()

    def group(ids_ref, t_next, half_next, t0, half):
        slots = [half * PEER_GROUP + i for i in range(PEER_GROUP)]
        for slot in slots:
            wait(slot)
        state = [None] * PEER_GROUP

        def stage(n, i):
            t, slot = t0 + i, slots[i]
            if n == 0:
                x = _fold(h_ref[pl.ds(t, 1), :]).astype(BF16)
                u = buf[slot, :, 0:FOLD, :].reshape(rows, LANES)
                state[i] = _dot_t(x, u)
            elif n == 1:
                part = _fold(jnp.sum(jnp.where(own, state[i], 0.0), axis=0, keepdims=True))
                for shift in (1, 2, 4, 8):
                    part = part + jnp.where(pos >= shift, pltpu.roll(part, shift, axis=1),
                                            pltpu.roll(part, LANES - FOLD + shift, axis=1))
                gate = _fold(g_ref[pl.ds(t, 1), :])
                state[i] = gate * (0.5 * part * (1.0 + lax.erf(part * (2.0 ** -0.5))))
            elif n == 2:
                a_own = jnp.where(own, _unfold(state[i]), 0.0).astype(BF16)
                v = buf[slot, :, FOLD:2 * FOLD, :].reshape(rows, LANES)
                state[i] = jnp.dot(a_own, v, preferred_element_type=F32)
            else:
                o_ref[pl.ds(t, 1), :] = _unfold(state[i])

        for n in range(stages):
            for i in range(PEER_GROUP):
                stage(n, i)
                start_burst(ids_ref, t_next + i, half_next * PEER_GROUP + i, n)

    @pl.when(step == 0)
    def _():
        for i in range(PEER_GROUP):
            for burst in range(stages):
                start_burst(e_ref, i, i, burst)

    def body(j, carry):
        group(e_ref, (j + 1) * PEER_GROUP, (j + 1) % 2, j * PEER_GROUP, j % 2)
        return carry

    lax.fori_loop(0, groups - 1, body, 0)
    group(e_next_ref, 0, 0, (groups - 1) * PEER_GROUP, (groups - 1) % 2)

    @pl.when(step == steps - 1)
    def _():
        for i in range(PEER_GROUP):
            wait(i)


def _peer_expert(experts, h, gates, uv_tab):
    n = h.shape[0]
    tb = min(PEER_TOKENS, n)
    steps = n // tb
    row = lambda i: (i, 0)
    return pl.pallas_call(
        functools.partial(_peer_expert_kernel, steps=steps),
        grid=(steps,),
        in_specs=[pl.BlockSpec((tb, PEER_SLOTS), row, memory_space=pltpu.SMEM),
                  pl.BlockSpec((tb, PEER_SLOTS), lambda i: (jnp.minimum(i + 1, steps - 1), 0),
                               memory_space=pltpu.SMEM),
                  pl.BlockSpec((tb, D_MODEL), row),
                  pl.BlockSpec((tb, PEER_SLOTS * FOLD), row),
                  pl.BlockSpec(memory_space=pl.ANY)],
        out_specs=pl.BlockSpec((tb, D_MODEL), row),
        out_shape=jax.ShapeDtypeStruct((n, D_MODEL), F32),
        scratch_shapes=[pltpu.VMEM((PEER_RING, PEER_SLOTS, 2 * FOLD, LANES), uv_tab.dtype),
                        pltpu.SemaphoreType.DMA((PEER_RING,))],
        compiler_params=_params("arbitrary"),
        name="peer_expert",
    )(experts, experts, h, gates, uv_tab)


def _tail_kernel(h_ref, y_ref, p_ref, g_ref, b_ref, wg_ref, wp_ref, o_ref):
    h2 = _layer_norm(DEEPNORM_ALPHA * h_ref[...] + y_ref[...], g_ref[...], b_ref[...])
    gate = jax.nn.sigmoid(jnp.dot(h2.astype(BF16), wg_ref[...], preferred_element_type=F32))
    emb = jnp.dot(p_ref[...].astype(BF16), wp_ref[...], preferred_element_type=F32)
    o_ref[...] = h2 + gate * emb


def _tail(h, y, p, g, b, wg_bf16, wp_bf16):
    n = h.shape[0]
    tm = min(ROW_TILE, n)
    row = lambda i: (i, 0)
    return pl.pallas_call(
        _tail_kernel,
        grid=(n // tm,),
        in_specs=[pl.BlockSpec((tm, D_MODEL), row), pl.BlockSpec((tm, D_MODEL), row),
                  pl.BlockSpec((tm, p.shape[1]), row), _resident(g.shape), _resident(b.shape),
                  _resident(wg_bf16.shape), _resident(wp_bf16.shape)],
        out_specs=pl.BlockSpec((tm, D_MODEL), row),
        out_shape=jax.ShapeDtypeStruct((n, D_MODEL), F32),
        compiler_params=_params("parallel"),
        name="tail",
    )(h, y, p, g, b, wg_bf16, wp_bf16)


def _rotary_tables(pos):
    half = HEAD_DIM // 2
    inv = ROPE_THETA ** (-jnp.arange(half, dtype=F32) / half)
    ang = pos.astype(F32)[:, None] * inv[None, :]
    cos, sin = jnp.cos(ang), jnp.sin(ang)
    return jnp.concatenate([cos, cos], axis=1), jnp.concatenate([-sin, sin], axis=1)


def _state_halo(state):
    return jnp.pad(state, ((0, 0), (SUBLANES - (CONV_WIDTH - 1), 0), (0, 0)))


def _layer_tail(x, att, conv, p, w):
    h1 = _out_proj(att, conv, x, w["w_out"], w["ln1_g"], w["ln1_b"])
    experts, gates = _peer_route(h1, w["peer_wq"], w["peer_subkeys"])
    y = _peer_expert(experts, h1, gates, w["peer_uv"])
    return _tail(h1, y, p, w["ln2_g"], w["ln2_b"], w["ple_gate"], w["ple_proj"])


def kernel(x_prompt, x_sample, state_attn_k, state_attn_v, state_conv, p_prompt, p_sample,
           w_in, w_conv, w_out, ln1_g, ln1_b, peer_wq, peer_subkeys, peer_u, peer_v,
           ln2_g, ln2_b, ple_gate, ple_proj):
    assert w_in.shape[0] == DEPTH
    bp, tp, _ = x_prompt.shape
    bs, ts, _ = x_sample.shape
    w = {
        "w_in": w_in[0].astype(BF16), "w_out": w_out[0].astype(BF16),
        "ln1_g": ln1_g, "ln1_b": ln1_b, "ln2_g": ln2_g, "ln2_b": ln2_b,
        "peer_wq": peer_wq[0].astype(BF16), "peer_subkeys": peer_subkeys[0].astype(BF16),
        "peer_uv": jnp.concatenate([peer_u.reshape(-1, FOLD, LANES), peer_v.reshape(-1, FOLD, LANES)],
                                   axis=1).astype(BF16),
        "ple_gate": ple_gate[0].astype(BF16), "ple_proj": ple_proj[0].astype(BF16),
    }
    wc = w_conv[0]

    xp = x_prompt.reshape(bp * tp, D_MODEL)
    cos_p, sin_p = _rotary_tables(jnp.arange(tp))
    q, k, v, z, gb = _in_proj(xp, w["w_in"], cos_p, sin_p)
    conv = _conv(z, gb, jnp.zeros((bp, SUBLANES, D_CONV), F32), wc, bp, tp)
    att = _attn_prompt(q, k, v, bp, tp)
    y_prompt = _layer_tail(xp, att, conv, p_prompt[0].reshape(bp * tp, -1), w).reshape(bp, tp, D_MODEL)
    new_k_prompt = k.reshape(1, bp, tp, ATT_HEADS, HEAD_DIM)
    new_v_prompt = v.reshape(1, bp, tp, ATT_HEADS, HEAD_DIM)
    new_conv_prompt = z.reshape(bp, tp, D_CONV)[:, tp - (CONV_WIDTH - 1):][None]

    xs = x_sample.reshape(bs * ts, D_MODEL)
    cos_s, sin_s = _rotary_tables(jnp.tile(PAST_LEN + jnp.arange(ts), bs))
    q, k, v, z, gb = _in_proj(xs, w["w_in"], cos_s, sin_s)
    conv = _conv(z, gb, _state_halo(state_conv[0]), wc, bs, ts)
    att, new_k_sample, new_v_sample = _attn_sample(q, k, v, state_attn_k, state_attn_v, bs, ts)
    y_sample = _layer_tail(xs, att, conv, p_sample[0].reshape(bs * ts, -1), w).reshape(bs, ts, D_MODEL)
    z_ext = jnp.concatenate([state_conv[0], z.reshape(bs, ts, D_CONV)], axis=1)
    new_conv_sample = z_ext[:, -(CONV_WIDTH - 1):][None]

    return (y_prompt, y_sample, new_k_prompt, new_v_prompt, new_conv_prompt,
            new_k_sample, new_v_sample, new_conv_sample)
```

```python
import functools
import math

import jax
import jax.numpy as jnp
from jax import lax
from jax.experimental import pallas as pl
from jax.experimental.pallas import tpu as pltpu

F32 = jnp.float32
BF16 = jnp.bfloat16
I32 = jnp.int32

D_MODEL = 2048
PAST_LEN = 16384
ATT_HEADS = 8
HEAD_DIM = 128
D_ATT = ATT_HEADS * HEAD_DIM
D_CONV = D_MODEL - D_ATT
CONV_WIDTH = 3
DILATIONS = (1, 4, 16)
BAND = 128
ROPE_THETA = 10000.0
PEER_HEADS = 8
N_SUBKEYS = 128
PEER_KEY_DIM = 256
PEER_TOPK = 16
PEER_SLOTS = PEER_HEADS * PEER_TOPK
DEPTH = 1
DEEPNORM_ALPHA = (2.0 * DEPTH) ** 0.25
LN_EPS = 1e-5

LANES = 128
SUBLANES = 8
FOLD = D_MODEL // LANES
ROW_TILE = 256
VMEM_LIMIT = 56 * 1024 * 1024
NEG_INF = float("-inf")


def _params(*sem):
    return pltpu.CompilerParams(dimension_semantics=sem, vmem_limit_bytes=VMEM_LIMIT)


def _resident(shape):
    nd = len(shape)
    return pl.BlockSpec(shape, lambda *_: (0,) * nd, pipeline_mode=pl.Buffered(1))


def _layer_norm(y, g, b):
    mu = jnp.mean(y, axis=-1, keepdims=True)
    c = y - mu
    var = jnp.mean(c * c, axis=-1, keepdims=True)
    return c * lax.rsqrt(var + LN_EPS) * g + b


def _in_proj_kernel(x_ref, w_ref, cos_ref, sin_ref, q_ref, k_ref, v_ref, z_ref, gb_ref):
    xb = x_ref[...].astype(BF16)
    cos = cos_ref[...]
    sin = sin_ref[...]

    def proj(c):
        return jnp.dot(xb, w_ref[:, c * D_ATT:(c + 1) * D_ATT], preferred_element_type=F32)

    def store_rotary(a, o_ref):
        for h in range(ATT_HEADS):
            ah = a[:, h * HEAD_DIM:(h + 1) * HEAD_DIM]
            o_ref[:, h * HEAD_DIM:(h + 1) * HEAD_DIM] = (
                ah * cos + pltpu.roll(ah, HEAD_DIM // 2, axis=1) * sin)

    store_rotary(proj(0), q_ref)
    store_rotary(proj(1), k_ref)
    v_ref[...] = proj(2)
    u = proj(3)
    gb_ref[...] = proj(4)
    z_ref[...] = proj(5) * u


def _in_proj(x, w_bf16, cos_tab, sin_tab):
    n = x.shape[0]
    tm = min(ROW_TILE, n)
    n_tab = cos_tab.shape[0] // tm
    row = lambda i: (i, 0)
    out = jax.ShapeDtypeStruct((n, D_ATT), F32)
    return pl.pallas_call(
        _in_proj_kernel,
        grid=(n // tm,),
        in_specs=[pl.BlockSpec((tm, D_MODEL), row),
                  _resident(w_bf16.shape),
                  pl.BlockSpec((tm, HEAD_DIM), lambda i: (i % n_tab, 0)),
                  pl.BlockSpec((tm, HEAD_DIM), lambda i: (i % n_tab, 0))],
        out_specs=[pl.BlockSpec((tm, D_ATT), row)] * 5,
        out_shape=[out] * 5,
        compiler_params=_params("parallel"),
        name="in_proj",
    )(x, w_bf16, cos_tab, sin_tab)


def _conv_kernel(z_ref, zprev_ref, st_ref, gb_ref, wc_ref, o_ref):
    halo = jnp.where(pl.program_id(1) == 0, st_ref[...], zprev_ref[...])
    z = z_ref[...]
    row = lax.broadcasted_iota(I32, z.shape, 0)
    h1 = halo[SUBLANES - 1:SUBLANES, :]
    h2 = halo[SUBLANES - 2:SUBLANES - 1, :]
    zm1 = jnp.where(row == 0, h1, pltpu.roll(z, 1, axis=0))
    zm2 = jnp.where(row == 0, h2, jnp.where(row == 1, h1, pltpu.roll(z, 2, axis=0)))
    wc = wc_ref[...]
    o_ref[...] = gb_ref[...] * (wc[0:1, :] * zm2 + wc[1:2, :] * zm1 + wc[2:3, :] * z)


def _conv(z, gb, state_halo, w_conv, batch, seq):
    tt = min(ROW_TILE, seq)
    per_seq = seq // tt
    halo_per_tile = tt // SUBLANES
    z3 = z.reshape(batch, seq, D_CONV)
    gb3 = gb.reshape(batch, seq, D_CONV)
    tile = pl.BlockSpec((None, tt, D_CONV), lambda b, i: (b, i, 0))
    out = pl.pallas_call(
        _conv_kernel,
        grid=(batch, per_seq),
        in_specs=[tile,
                  pl.BlockSpec((None, SUBLANES, D_CONV),
                               lambda b, i: (b, jnp.maximum(i * halo_per_tile - 1, 0), 0)),
                  pl.BlockSpec((None, SUBLANES, D_CONV), lambda b, i: (b, 0, 0)),
                  tile,
                  pl.BlockSpec((CONV_WIDTH, D_CONV), lambda b, i: (0, 0))],
        out_specs=tile,
        out_shape=jax.ShapeDtypeStruct((batch, seq, D_CONV), F32),
        compiler_params=_params("parallel", "parallel"),
        name="conv",
    )(z3, z3, state_halo, gb3, w_conv)
    return out.reshape(batch * seq, D_CONV)


def _dot_t(a, b):
    return lax.dot_general(a, b, (((1,), (1,)), ((), ())), preferred_element_type=F32)


ATTN_UNROLL = 4


def _attn_prompt_kernel(q_ref, k_ref, v_ref, o_ref, acc_sc, m_sc, l_sc):
    seq = q_ref.shape[0]
    scale = HEAD_DIM ** -0.5
    qi = lax.broadcasted_iota(I32, (BAND, BAND), 0)
    kj = lax.broadcasted_iota(I32, (BAND, BAND), 1)
    cur_ok = kj <= qi
    prev_ok = kj >= qi

    for bi, d in enumerate(DILATIONS):
        nb = seq // (d * BAND)

        def block(blk, carry, bi=bi, d=d, nb=nb):
            r = blk // nb
            n = blk % nb
            start = r + n * (d * BAND)
            prev_start = jnp.maximum(start - d * BAND, 0)
            rows = pl.ds(start, BAND, stride=d) if d > 1 else pl.ds(pl.multiple_of(start, BAND), BAND)
            prows = pl.ds(prev_start, BAND, stride=d) if d > 1 else pl.ds(pl.multiple_of(prev_start, BAND), BAND)
            q = q_ref[rows, :].astype(BF16)
            s_cur = _dot_t(q, k_ref[rows, :].astype(BF16)) * scale
            s_prev = _dot_t(q, k_ref[prows, :].astype(BF16)) * scale
            s_cur = jnp.where(cur_ok, s_cur, NEG_INF)
            s_prev = jnp.where(prev_ok, s_prev, NEG_INF) + jnp.where(n > 0, 0.0, NEG_INF)
            m = jnp.maximum(jnp.max(s_cur, axis=1, keepdims=True),
                            jnp.max(s_prev, axis=1, keepdims=True))
            p_cur = jnp.exp(s_cur - m)
            p_prev = jnp.exp(s_prev - m)
            l = jnp.sum(p_cur, axis=1, keepdims=True) + jnp.sum(p_prev, axis=1, keepdims=True)
            acc = (jnp.dot(p_cur.astype(BF16), v_ref[rows, :].astype(BF16), preferred_element_type=F32)
                   + jnp.dot(p_prev.astype(BF16), v_ref[prows, :].astype(BF16), preferred_element_type=F32))
            acc_sc[bi, rows, :] = acc
            m_sc[bi, rows, :] = jnp.broadcast_to(m, (BAND, HEAD_DIM))
            l_sc[bi, rows, :] = jnp.broadcast_to(l, (BAND, HEAD_DIM))
            return carry

        lax.fori_loop(0, d * nb, block, 0, unroll=ATTN_UNROLL)

    big_m = jnp.maximum(jnp.maximum(m_sc[0], m_sc[1]), m_sc[2])
    num = jnp.zeros((seq, HEAD_DIM), F32)
    den = jnp.zeros((seq, HEAD_DIM), F32)
    for bi in range(len(DILATIONS)):
        w = jnp.exp(m_sc[bi] - big_m)
        num = num + w * acc_sc[bi]
        den = den + w * l_sc[bi]
    o_ref[...] = num / den


def _attn_prompt(q, k, v, batch, seq):
    q3, k3, v3 = (a.reshape(batch, seq, D_ATT) for a in (q, k, v))
    head = pl.BlockSpec((None, seq, HEAD_DIM), lambda b, h: (b, 0, h))
    nbr = len(DILATIONS)
    out = pl.pallas_call(
        _attn_prompt_kernel,
        grid=(batch, ATT_HEADS),
        in_specs=[head, head, head],
        out_specs=head,
        out_shape=jax.ShapeDtypeStruct((batch, seq, D_ATT), F32),
        scratch_shapes=[pltpu.VMEM((nbr, seq, HEAD_DIM), F32)] * 3,
        compiler_params=_params("parallel", "parallel"),
        name="attn_prompt",
    )(q3, k3, v3)
    return out.reshape(batch * seq, D_ATT)


def _branch_count(rel):
    cnt = jnp.zeros(rel.shape, F32)
    for d in DILATIONS:
        hit = (rel >= 0) & (rel <= d * BAND) & ((rel & (d - 1)) == 0)
        cnt = cnt + jnp.where(hit, 1.0, 0.0)
    return cnt


SAMPLE_CHUNK = 512
M_FLOOR = -1e30


def _attn_sample_kernel(q_ref, kn_ref, vn_ref, kn3_ref, vn3_ref, ks_ref, ks_next_ref, vs_ref, vs_next_ref,
                        o_ref, ko_ref, vo_ref, m_sc, l_sc, acc_sc, *, n_buf, chunks):
    c = pl.program_id(1)
    rows = ks_ref.shape[0]
    t_new = q_ref.shape[0]
    scale = HEAD_DIM ** -0.5
    log_heads = int(math.log2(ATT_HEADS))
    log_new = int(math.log2(t_new))
    assert ATT_HEADS == 1 << log_heads and t_new == 1 << log_new

    def by_head(ref):
        return jnp.concatenate([ref[:, h * HEAD_DIM:(h + 1) * HEAD_DIM] for h in range(ATT_HEADS)], axis=0)

    @pl.when(c == 0)
    def _():
        m_sc[...] = jnp.full(m_sc.shape, M_FLOOR, F32)
        l_sc[...] = jnp.zeros(l_sc.shape, F32)
        acc_sc[...] = jnp.zeros(acc_sc.shape, F32)

    q = by_head(q_ref).astype(BF16)

    def accumulate(k2, v2, key_head, rel):
        q_head = lax.shift_right_logical(lax.broadcasted_iota(I32, rel.shape, 0), log_new)
        cnt = jnp.where(key_head == q_head, _branch_count(rel), 0.0)
        s = jnp.where(cnt > 0, _dot_t(q, k2.astype(BF16)) * scale, NEG_INF)
        m_old = m_sc[...]
        m_new = jnp.maximum(m_old, jnp.max(s, axis=1, keepdims=True))
        alpha = jnp.exp(m_old - m_new)
        p = cnt * jnp.exp(s - m_new)
        l_sc[...] = alpha * l_sc[...] + jnp.sum(p, axis=1, keepdims=True)
        acc_sc[...] = alpha * acc_sc[...] + jnp.dot(p.astype(BF16), v2.astype(BF16),
                                                    preferred_element_type=F32)
        m_sc[...] = m_new

    ks = ks_ref[...]
    vs = vs_ref[...]
    n = rows * ATT_HEADS
    col = lax.broadcasted_iota(I32, (ATT_HEADS * t_new, n), 1)
    qrow = lax.broadcasted_iota(I32, (ATT_HEADS * t_new, n), 0)
    kpos = c * rows + lax.shift_right_logical(col, log_heads)
    qpos = n_buf + (qrow & (t_new - 1))
    accumulate(ks.reshape(n, HEAD_DIM), vs.reshape(n, HEAD_DIM), col & (ATT_HEADS - 1), qpos - kpos)

    last = c == chunks - 1
    ko_ref[0:rows - t_new] = ks[t_new:]
    vo_ref[0:rows - t_new] = vs[t_new:]
    ko_ref[rows - t_new:rows] = jnp.where(last, kn3_ref[...], ks_next_ref[...])
    vo_ref[rows - t_new:rows] = jnp.where(last, vn3_ref[...], vs_next_ref[...])

    @pl.when(last)
    def _():
        m2 = ATT_HEADS * t_new
        ncol = lax.broadcasted_iota(I32, (m2, m2), 1)
        nrow = lax.broadcasted_iota(I32, (m2, m2), 0)
        accumulate(by_head(kn_ref), by_head(vn_ref), lax.shift_right_logical(ncol, log_new),
                   (nrow & (t_new - 1)) - (ncol & (t_new - 1)))
        out = acc_sc[...] / l_sc[...]
        for h in range(ATT_HEADS):
            o_ref[:, h * HEAD_DIM:(h + 1) * HEAD_DIM] = out[h * t_new:(h + 1) * t_new, :]


def _attn_sample(q, k_new, v_new, k_state, v_state, batch, t_new):
    n_buf = k_state.shape[2]
    rows = min(SAMPLE_CHUNK, n_buf)
    chunks = n_buf // rows
    assert t_new == SUBLANES and rows % t_new == 0
    kn3 = k_new.reshape(batch * t_new, ATT_HEADS, HEAD_DIM)
    vn3 = v_new.reshape(batch * t_new, ATT_HEADS, HEAD_DIM)
    new = pl.BlockSpec((t_new, D_ATT), lambda b, c: (b, 0))
    new3 = pl.BlockSpec((t_new, ATT_HEADS, HEAD_DIM), lambda b, c: (b, 0, 0))
    buf = pl.BlockSpec((None, None, rows, ATT_HEADS, HEAD_DIM), lambda b, c: (0, b, c, 0, 0))
    per = rows // t_new
    nxt = pl.BlockSpec((None, None, t_new, ATT_HEADS, HEAD_DIM),
                       lambda b, c: (0, b, jnp.minimum((c + 1) * per, n_buf // t_new - 1), 0, 0))
    buf_shape = jax.ShapeDtypeStruct(k_state.shape, F32)
    qrows = ATT_HEADS * t_new
    return pl.pallas_call(
        functools.partial(_attn_sample_kernel, n_buf=n_buf, chunks=chunks),
        grid=(batch, chunks),
        in_specs=[new, new, new, new3, new3, buf, nxt, buf, nxt],
        out_specs=[new, buf, buf],
        out_shape=[jax.ShapeDtypeStruct((batch * t_new, D_ATT), F32), buf_shape, buf_shape],
        scratch_shapes=[pltpu.VMEM((qrows, 1), F32), pltpu.VMEM((qrows, 1), F32),
                        pltpu.VMEM((qrows, HEAD_DIM), F32)],
        compiler_params=_params("parallel", "arbitrary"),
        name="attn_sample",
    )(q, k_new, v_new, kn3, vn3, k_state, k_state, v_state, v_state)


def _out_proj_kernel(att_ref, conv_ref, x_ref, w_ref, g_ref, b_ref, o_ref):
    mixed = (jnp.dot(att_ref[...].astype(BF16), w_ref[0:D_ATT, :], preferred_element_type=F32)
             + jnp.dot(conv_ref[...].astype(BF16), w_ref[D_ATT:D_MODEL, :], preferred_element_type=F32))
    o_ref[...] = _layer_norm(DEEPNORM_ALPHA * x_ref[...] + mixed, g_ref[...], b_ref[...])


def _out_proj(att, conv, x, w_bf16, g, b):
    n = x.shape[0]
    tm = min(ROW_TILE, n)
    row = lambda i: (i, 0)
    return pl.pallas_call(
        _out_proj_kernel,
        grid=(n // tm,),
        in_specs=[pl.BlockSpec((tm, D_ATT), row), pl.BlockSpec((tm, D_CONV), row),
                  pl.BlockSpec((tm, D_MODEL), row), _resident(w_bf16.shape),
                  _resident(g.shape), _resident(b.shape)],
        out_specs=pl.BlockSpec((tm, D_MODEL), row),
        out_shape=jax.ShapeDtypeStruct((n, D_MODEL), F32),
        compiler_params=_params("parallel"),
        name="out_proj",
    )(att, conv, x, w_bf16, g, b)


ID_NONE = float(2 ** 24)


def _top_rows(s, ids, k):
    vals, sel = [], []
    for _ in range(k):
        m = jnp.max(s, axis=0, keepdims=True)
        i = jnp.min(jnp.where(s == m, ids, ID_NONE), axis=0, keepdims=True)
        vals.append(m)
        sel.append(i)
        s = jnp.where(ids == i, NEG_INF, s)
    return jnp.concatenate(vals, axis=0), jnp.concatenate(sel, axis=0)


def _pair_candidates(v1, v2, sub):
    assert PEER_TOPK == 2 * SUBLANES
    vals = [v1[0:1, :] + v2]
    ids = [sub, sub + SUBLANES]
    for i in range(1, SUBLANES):
        vals.append(v1[i:i + 1, :] + v2[0:SUBLANES, :])
        ids.append(sub + i * PEER_TOPK)
    vals.append(v1[SUBLANES:, :] + v2[0:1, :])
    ids.append((sub + SUBLANES) * PEER_TOPK)
    return jnp.concatenate(vals, axis=0), jnp.concatenate(ids, axis=0)


def _take_rows(table, idx):
    out = jnp.zeros(idx.shape, table.dtype)
    for i in range(table.shape[0]):
        out = jnp.where(idx == i, table[i:i + 1, :], out)
    return out


def _peer_route_kernel(h_ref, wq_ref, sk_ref, e_ref, g_ref):
    hb = h_ref[...].astype(BF16)
    tokens = hb.shape[0]
    half = PEER_KEY_DIM // 2
    key_ids = lax.broadcasted_iota(I32, (N_SUBKEYS, tokens), 0).astype(F32)
    sub = lax.broadcasted_iota(I32, (SUBLANES, tokens), 0).astype(F32)
    experts, gates = [], []
    for head in range(PEER_HEADS):
        top = []
        for p in range(2):
            c0 = head * PEER_KEY_DIM + p * half
            qhp = jnp.dot(hb, wq_ref[:, c0:c0 + half], preferred_element_type=F32)
            sc = _dot_t(sk_ref[p], qhp.astype(BF16))
            top.append(_top_rows(sc, key_ids, PEER_TOPK))
        (v1, i1), (v2, i2) = top
        cand, cand_ids = _pair_candidates(v1, v2, sub)
        c_top, c_id = _top_rows(cand, cand_ids, PEER_TOPK)
        rank1 = jnp.floor(c_id * (1.0 / PEER_TOPK))
        rank2 = c_id - rank1 * PEER_TOPK
        experts.append(_take_rows(i1, rank1) * N_SUBKEYS + _take_rows(i2, rank2))
        ex = jnp.exp(c_top - jnp.max(c_top, axis=0, keepdims=True))
        gates.append(ex / jnp.sum(ex, axis=0, keepdims=True))
    e_ref[...] = jnp.concatenate(experts, axis=0).T.astype(I32)
    g_ref[...] = jnp.concatenate(gates, axis=0).T


def _peer_route(h, wq_bf16, subkeys_bf16):
    n = h.shape[0]
    tm = min(ROW_TILE, n)
    row = lambda i: (i, 0)
    return pl.pallas_call(
        _peer_route_kernel,
        grid=(n // tm,),
        in_specs=[pl.BlockSpec((tm, D_MODEL), row), _resident(wq_bf16.shape),
                  _resident(subkeys_bf16.shape)],
        out_specs=[pl.BlockSpec((tm, PEER_SLOTS), row)] * 2,
        out_shape=[jax.ShapeDtypeStruct((n, PEER_SLOTS), I32),
                   jax.ShapeDtypeStruct((n, PEER_SLOTS), F32)],
        compiler_params=_params("parallel"),
        name="peer_route",
    )(h, wq_bf16, subkeys_bf16)


PEER_TOKENS = 64
PEER_RING = 8


def _fold(row):
    return jnp.concatenate([row[:, j * LANES:(j + 1) * LANES] for j in range(FOLD)], axis=0)


def _unfold(folded):
    return jnp.concatenate([folded[j:j + 1, :] for j in range(FOLD)], axis=1)


def _sublane_sums(vregs):
    sub = lax.broadcasted_iota(I32, (SUBLANES, LANES), 0)
    roll = lambda a, k: pltpu.roll(a, k, axis=0)
    p = [vregs[i] for i in (3, 2, 1, 0, 7, 6, 5, 4)]
    m = [jnp.where(sub < 4, p[i] + roll(p[i], 4), p[i + 4] + roll(p[i + 4], 4)) for i in range(4)]
    n = [jnp.where((sub & 3) >= 2, m[i] + roll(m[i], 2), roll(m[i + 2] + roll(m[i + 2], 2), 6))
         for i in range(2)]
    return jnp.where((sub & 1) == 1, n[0] + roll(n[0], 1), roll(n[1] + roll(n[1], 1), 7))


def _peer_expert_kernel(e_ref, e_next_ref, h_ref, g_ref, uv_hbm, o_ref, buf, sem, *, steps):
    tokens = h_ref.shape[0]
    ahead = PEER_RING - 1
    assert tokens % PEER_RING == 0 and tokens > ahead and FOLD == 2 * SUBLANES
    step = pl.program_id(0)
    eye = (lax.broadcasted_iota(I32, (PEER_SLOTS, PEER_SLOTS), 0)
           == lax.broadcasted_iota(I32, (PEER_SLOTS, PEER_SLOTS), 1))

    pieces = 2 * PEER_SLOTS // SUBLANES
    per_piece = PEER_SLOTS // pieces

    def start_piece(ids_ref, t, slot, piece):
        for k in range(piece * per_piece, (piece + 1) * per_piece):
            pltpu.make_async_copy(uv_hbm.at[ids_ref[t, k]], buf.at[slot, k], sem.at[slot]).start(priority=k % 2)

    def wait(slot):
        pltpu.make_async_copy(uv_hbm.at[pl.ds(0, PEER_SLOTS)], buf.at[slot], sem.at[slot]).wait()

    def token(ids_ref, t_ahead, slot_ahead, t, slot):
        wait(slot)
        x = _fold(h_ref[pl.ds(t, 1), :])
        sums = []
        for c in range(pieces // 2):
            start_piece(ids_ref, t_ahead, slot_ahead, c)
            rows = slice(c * SUBLANES, (c + 1) * SUBLANES)
            prod = buf[slot, rows, 0:FOLD, :].astype(F32) * x[None]
            sums.append(_sublane_sums([prod[i, 0:SUBLANES, :] + prod[i, SUBLANES:FOLD, :]
                                       for i in range(SUBLANES)]))
        hid = jnp.sum(jnp.concatenate(sums, axis=0), axis=1, keepdims=True)
        gate = jnp.sum(jnp.where(eye, g_ref[pl.ds(t, 1), :], 0.0), axis=1, keepdims=True)
        a = gate * (0.5 * hid * (1.0 + lax.erf(hid * (2.0 ** -0.5))))
        a = jnp.broadcast_to(a, (PEER_SLOTS, LANES))
        y = jnp.zeros((FOLD, LANES), F32)
        for c in range(pieces // 2):
            start_piece(ids_ref, t_ahead, slot_ahead, pieces // 2 + c)
            rows = slice(c * SUBLANES, (c + 1) * SUBLANES)
            y = y + jnp.sum(a[rows][:, None, :] * buf[slot, rows, FOLD:2 * FOLD, :].astype(F32), axis=0)
        o_ref[pl.ds(t, 1), :] = _unfold(y)

    @pl.when(step == 0)
    def _():
        for t in range(ahead):
            for piece in range(pieces):
                start_piece(e_ref, t, t, piece)

    def body(t, carry):
        token(e_ref, t + ahead, (t + ahead) % PEER_RING, t, t % PEER_RING)
        return carry

    lax.fori_loop(0, tokens - ahead, body, 0)
    for j in range(ahead):
        t = tokens - ahead + j
        token(e_next_ref, j, j % PEER_RING, t, t % PEER_RING)

    @pl.when(step == steps - 1)
    def _():
        for j in range(ahead):
            wait(j % PEER_RING)


def _peer_expert(experts, h, gates, uv_tab):
    n = h.shape[0]
    tb = min(PEER_TOKENS, n)
    steps = n // tb
    row = lambda i: (i, 0)
    return pl.pallas_call(
        functools.partial(_peer_expert_kernel, steps=steps),
        grid=(steps,),
        in_specs=[pl.BlockSpec((tb, PEER_SLOTS), row, memory_space=pltpu.SMEM),
                  pl.BlockSpec((tb, PEER_SLOTS), lambda i: (jnp.minimum(i + 1, steps - 1), 0),
                               memory_space=pltpu.SMEM),
                  pl.BlockSpec((tb, D_MODEL), row),
                  pl.BlockSpec((tb, PEER_SLOTS), row),
                  pl.BlockSpec(memory_space=pl.ANY)],
        out_specs=pl.BlockSpec((tb, D_MODEL), row),
        out_shape=jax.ShapeDtypeStruct((n, D_MODEL), F32),
        scratch_shapes=[pltpu.VMEM((PEER_RING, PEER_SLOTS, 2 * FOLD, LANES), uv_tab.dtype),
                        pltpu.SemaphoreType.DMA((PEER_RING,))],
        compiler_params=_params("arbitrary"),
        name="peer_expert",
    )(experts, experts, h, gates, uv_tab)


def _tail_kernel(h_ref, y_ref, p_ref, g_ref, b_ref, wg_ref, wp_ref, o_ref):
    h2 = _layer_norm(DEEPNORM_ALPHA * h_ref[...] + y_ref[...], g_ref[...], b_ref[...])
    gate = jax.nn.sigmoid(jnp.dot(h2.astype(BF16), wg_ref[...], preferred_element_type=F32))
    emb = jnp.dot(p_ref[...].astype(BF16), wp_ref[...], preferred_element_type=F32)
    o_ref[...] = h2 + gate * emb


def _tail(h, y, p, g, b, wg_bf16, wp_bf16):
    n = h.shape[0]
    tm = min(ROW_TILE, n)
    row = lambda i: (i, 0)
    return pl.pallas_call(
        _tail_kernel,
        grid=(n // tm,),
        in_specs=[pl.BlockSpec((tm, D_MODEL), row), pl.BlockSpec((tm, D_MODEL), row),
                  pl.BlockSpec((tm, p.shape[1]), row), _resident(g.shape), _resident(b.shape),
                  _resident(wg_bf16.shape), _resident(wp_bf16.shape)],
        out_specs=pl.BlockSpec((tm, D_MODEL), row),
        out_shape=jax.ShapeDtypeStruct((n, D_MODEL), F32),
        compiler_params=_params("parallel"),
        name="tail",
    )(h, y, p, g, b, wg_bf16, wp_bf16)


def _rotary_tables(pos):
    half = HEAD_DIM // 2
    inv = ROPE_THETA ** (-jnp.arange(half, dtype=F32) / half)
    ang = pos.astype(F32)[:, None] * inv[None, :]
    cos, sin = jnp.cos(ang), jnp.sin(ang)
    return jnp.concatenate([cos, cos], axis=1), jnp.concatenate([-sin, sin], axis=1)


def _state_halo(state):
    return jnp.pad(state, ((0, 0), (SUBLANES - (CONV_WIDTH - 1), 0), (0, 0)))


def _layer_tail(x, att, conv, p, w):
    h1 = _out_proj(att, conv, x, w["w_out"], w["ln1_g"], w["ln1_b"])
    experts, gates = _peer_route(h1, w["peer_wq"], w["peer_subkeys"])
    y = _peer_expert(experts, h1, gates, w["peer_uv"])
    return _tail(h1, y, p, w["ln2_g"], w["ln2_b"], w["ple_gate"], w["ple_proj"])


def kernel(x_prompt, x_sample, state_attn_k, state_attn_v, state_conv, p_prompt, p_sample,
           w_in, w_conv, w_out, ln1_g, ln1_b, peer_wq, peer_subkeys, peer_u, peer_v,
           ln2_g, ln2_b, ple_gate, ple_proj):
    assert w_in.shape[0] == DEPTH
    bp, tp, _ = x_prompt.shape
    bs, ts, _ = x_sample.shape
    w = {
        "w_in": w_in[0].astype(BF16), "w_out": w_out[0].astype(BF16),
        "ln1_g": ln1_g, "ln1_b": ln1_b, "ln2_g": ln2_g, "ln2_b": ln2_b,
        "peer_wq": peer_wq[0].astype(BF16), "peer_subkeys": peer_subkeys[0].astype(BF16),
        "peer_uv": jnp.concatenate([peer_u.reshape(-1, FOLD, LANES), peer_v.reshape(-1, FOLD, LANES)],
                                   axis=1).astype(BF16),
        "ple_gate": ple_gate[0].astype(BF16), "ple_proj": ple_proj[0].astype(BF16),
    }
    wc = w_conv[0]

    xp = x_prompt.reshape(bp * tp, D_MODEL)
    cos_p, sin_p = _rotary_tables(jnp.arange(tp))
    q, k, v, z, gb = _in_proj(xp, w["w_in"], cos_p, sin_p)
    conv = _conv(z, gb, jnp.zeros((bp, SUBLANES, D_CONV), F32), wc, bp, tp)
    att = _attn_prompt(q, k, v, bp, tp)
    y_prompt = _layer_tail(xp, att, conv, p_prompt[0].reshape(bp * tp, -1), w).reshape(bp, tp, D_MODEL)
    new_k_prompt = k.reshape(1, bp, tp, ATT_HEADS, HEAD_DIM)
    new_v_prompt = v.reshape(1, bp, tp, ATT_HEADS, HEAD_DIM)
    new_conv_prompt = z.reshape(bp, tp, D_CONV)[:, tp - (CONV_WIDTH - 1):][None]

    xs = x_sample.reshape(bs * ts, D_MODEL)
    cos_s, sin_s = _rotary_tables(jnp.tile(PAST_LEN + jnp.arange(ts), bs))
    q, k, v, z, gb = _in_proj(xs, w["w_in"], cos_s, sin_s)
    conv = _conv(z, gb, _state_halo(state_conv[0]), wc, bs, ts)
    att, new_k_sample, new_v_sample = _attn_sample(q, k, v, state_attn_k, state_attn_v, bs, ts)
    y_sample = _layer_tail(xs, att, conv, p_sample[0].reshape(bs * ts, -1), w).reshape(bs, ts, D_MODEL)
    z_ext = jnp.concatenate([state_conv[0], z.reshape(bs, ts, D_CONV)], axis=1)
    new_conv_sample = z_ext[:, -(CONV_WIDTH - 1):][None]

    return (y_prompt, y_sample, new_k_prompt, new_v_prompt, new_conv_prompt,
            new_k_sample, new_v_sample, new_conv_sample)
```

```python
import functools
import math

import jax
import jax.numpy as jnp
from jax import lax
from jax.experimental import pallas as pl
from jax.experimental.pallas import tpu as pltpu

F32 = jnp.float32
BF16 = jnp.bfloat16
I32 = jnp.int32

D_MODEL = 2048
PAST_LEN = 16384
ATT_HEADS = 8
HEAD_DIM = 128
D_ATT = ATT_HEADS * HEAD_DIM
D_CONV = D_MODEL - D_ATT
CONV_WIDTH = 3
DILATIONS = (1, 4, 16)
BAND = 128
ROPE_THETA = 10000.0
PEER_HEADS = 8
N_SUBKEYS = 128
PEER_KEY_DIM = 256
PEER_TOPK = 16
PEER_SLOTS = PEER_HEADS * PEER_TOPK
DEPTH = 1
DEEPNORM_ALPHA = (2.0 * DEPTH) ** 0.25
LN_EPS = 1e-5

LANES = 128
SUBLANES = 8
FOLD = D_MODEL // LANES
ROW_TILE = 256
VMEM_LIMIT = 56 * 1024 * 1024
NEG_INF = float("-inf")


def _params(*sem):
    return pltpu.CompilerParams(dimension_semantics=sem, vmem_limit_bytes=VMEM_LIMIT)


def _resident(shape):
    nd = len(shape)
    return pl.BlockSpec(shape, lambda *_: (0,) * nd, pipeline_mode=pl.Buffered(1))


def _layer_norm(y, g, b):
    mu = jnp.mean(y, axis=-1, keepdims=True)
    c = y - mu
    var = jnp.mean(c * c, axis=-1, keepdims=True)
    return c * lax.rsqrt(var + LN_EPS) * g + b


def _in_proj_kernel(x_ref, w_ref, cos_ref, sin_ref, q_ref, k_ref, v_ref, z_ref, gb_ref):
    xb = x_ref[...].astype(BF16)
    cos = cos_ref[...]
    sin = sin_ref[...]

    def proj(c):
        return jnp.dot(xb, w_ref[:, c * D_ATT:(c + 1) * D_ATT], preferred_element_type=F32)

    def store_rotary(a, o_ref):
        for h in range(ATT_HEADS):
            ah = a[:, h * HEAD_DIM:(h + 1) * HEAD_DIM]
            o_ref[:, h * HEAD_DIM:(h + 1) * HEAD_DIM] = (
                ah * cos + pltpu.roll(ah, HEAD_DIM // 2, axis=1) * sin)

    store_rotary(proj(0), q_ref)
    store_rotary(proj(1), k_ref)
    v_ref[...] = proj(2)
    u = proj(3)
    gb_ref[...] = proj(4)
    z_ref[...] = proj(5) * u


def _in_proj(x, w_bf16, cos_tab, sin_tab):
    n = x.shape[0]
    tm = min(ROW_TILE, n)
    n_tab = cos_tab.shape[0] // tm
    row = lambda i: (i, 0)
    out = jax.ShapeDtypeStruct((n, D_ATT), F32)
    return pl.pallas_call(
        _in_proj_kernel,
        grid=(n // tm,),
        in_specs=[pl.BlockSpec((tm, D_MODEL), row),
                  _resident(w_bf16.shape),
                  pl.BlockSpec((tm, HEAD_DIM), lambda i: (i % n_tab, 0)),
                  pl.BlockSpec((tm, HEAD_DIM), lambda i: (i % n_tab, 0))],
        out_specs=[pl.BlockSpec((tm, D_ATT), row)] * 5,
        out_shape=[out] * 5,
        compiler_params=_params("parallel"),
        name="in_proj",
    )(x, w_bf16, cos_tab, sin_tab)


def _conv_kernel(z_ref, zprev_ref, st_ref, gb_ref, wc_ref, o_ref):
    halo = jnp.where(pl.program_id(1) == 0, st_ref[...], zprev_ref[...])
    z = z_ref[...]
    row = lax.broadcasted_iota(I32, z.shape, 0)
    h1 = halo[SUBLANES - 1:SUBLANES, :]
    h2 = halo[SUBLANES - 2:SUBLANES - 1, :]
    zm1 = jnp.where(row == 0, h1, pltpu.roll(z, 1, axis=0))
    zm2 = jnp.where(row == 0, h2, jnp.where(row == 1, h1, pltpu.roll(z, 2, axis=0)))
    wc = wc_ref[...]
    o_ref[...] = gb_ref[...] * (wc[0:1, :] * zm2 + wc[1:2, :] * zm1 + wc[2:3, :] * z)


def _conv(z, gb, state_halo, w_conv, batch, seq):
    tt = min(ROW_TILE, seq)
    per_seq = seq // tt
    halo_per_tile = tt // SUBLANES
    z3 = z.reshape(batch, seq, D_CONV)
    gb3 = gb.reshape(batch, seq, D_CONV)
    tile = pl.BlockSpec((None, tt, D_CONV), lambda b, i: (b, i, 0))
    out = pl.pallas_call(
        _conv_kernel,
        grid=(batch, per_seq),
        in_specs=[tile,
                  pl.BlockSpec((None, SUBLANES, D_CONV),
                               lambda b, i: (b, jnp.maximum(i * halo_per_tile - 1, 0), 0)),
                  pl.BlockSpec((None, SUBLANES, D_CONV), lambda b, i: (b, 0, 0)),
                  tile,
                  pl.BlockSpec((CONV_WIDTH, D_CONV), lambda b, i: (0, 0))],
        out_specs=tile,
        out_shape=jax.ShapeDtypeStruct((batch, seq, D_CONV), F32),
        compiler_params=_params("parallel", "parallel"),
        name="conv",
    )(z3, z3, state_halo, gb3, w_conv)
    return out.reshape(batch * seq, D_CONV)


def _dot_t(a, b):
    return lax.dot_general(a, b, (((1,), (1,)), ((), ())), preferred_element_type=F32)


ATTN_UNROLL = 4


def _attn_prompt_kernel(q_ref, k_ref, v_ref, o_ref, acc_sc, m_sc, l_sc):
    seq = q_ref.shape[0]
    scale = HEAD_DIM ** -0.5
    qi = lax.broadcasted_iota(I32, (BAND, 2 * BAND), 0)
    kj = lax.broadcasted_iota(I32, (BAND, 2 * BAND), 1)
    band_ok = (kj >= qi) & (kj <= qi + BAND)
    is_prev = kj < BAND
    own_ok = (lax.broadcasted_iota(I32, (BAND, BAND), 1) <= lax.broadcasted_iota(I32, (BAND, BAND), 0))

    for bi, d in enumerate(DILATIONS):
        nb = seq // (d * BAND)

        def block(blk, carry, bi=bi, d=d, nb=nb):
            r = blk // nb
            n = blk % nb
            start = r + n * (d * BAND)
            rows = pl.ds(start, BAND, stride=d) if d > 1 else pl.ds(pl.multiple_of(start, BAND), BAND)
            q = q_ref[rows, :].astype(BF16)
            if nb > 1:
                prev_start = jnp.maximum(start - d * BAND, 0)
                prows = (pl.ds(prev_start, BAND, stride=d) if d > 1
                         else pl.ds(pl.multiple_of(prev_start, BAND), BAND))
                keys = jnp.concatenate([k_ref[prows, :], k_ref[rows, :]], axis=0).astype(BF16)
                vals = jnp.concatenate([v_ref[prows, :], v_ref[rows, :]], axis=0)
                s = (jnp.where(band_ok, _dot_t(q, keys) * scale, NEG_INF)
                     + jnp.where(is_prev, jnp.where(n > 0, 0.0, NEG_INF), 0.0))
            else:
                keys = k_ref[rows, :].astype(BF16)
                vals = v_ref[rows, :]
                s = jnp.where(own_ok, _dot_t(q, keys) * scale, NEG_INF)
            m = jnp.max(s, axis=1, keepdims=True)
            p = jnp.exp(s - m).astype(BF16)
            vals_ones = jnp.concatenate([vals, jnp.ones(vals.shape, F32)], axis=1).astype(BF16)
            ext = jnp.dot(p, vals_ones, preferred_element_type=F32)
            acc_sc[bi, rows, :] = ext[:, :HEAD_DIM]
            l_sc[bi, rows, :] = ext[:, HEAD_DIM:]
            m_sc[bi, rows, :] = jnp.broadcast_to(m, (BAND, HEAD_DIM))
            return carry

        lax.fori_loop(0, d * nb, block, 0, unroll=ATTN_UNROLL)

    big_m = jnp.maximum(jnp.maximum(m_sc[0], m_sc[1]), m_sc[2])
    num = jnp.zeros((seq, HEAD_DIM), F32)
    den = jnp.zeros((seq, HEAD_DIM), F32)
    for bi in range(len(DILATIONS)):
        w = jnp.exp(m_sc[bi] - big_m)
        num = num + w * acc_sc[bi]
        den = den + w * l_sc[bi]
    o_ref[...] = num / den


def _attn_prompt(q, k, v, batch, seq):
    q3, k3, v3 = (a.reshape(batch, seq, D_ATT) for a in (q, k, v))
    head = pl.BlockSpec((None, seq, HEAD_DIM), lambda b, h: (b, 0, h))
    nbr = len(DILATIONS)
    out = pl.pallas_call(
        _attn_prompt_kernel,
        grid=(batch, ATT_HEADS),
        in_specs=[head, head, head],
        out_specs=head,
        out_shape=jax.ShapeDtypeStruct((batch, seq, D_ATT), F32),
        scratch_shapes=[pltpu.VMEM((nbr, seq, HEAD_DIM), F32)] * 3,
        compiler_params=_params("parallel", "parallel"),
        name="attn_prompt",
    )(q3, k3, v3)
    return out.reshape(batch * seq, D_ATT)


def _branch_count(rel):
    cnt = jnp.zeros(rel.shape, F32)
    for d in DILATIONS:
        hit = (rel >= 0) & (rel <= d * BAND) & ((rel & (d - 1)) == 0)
        cnt = cnt + jnp.where(hit, 1.0, 0.0)
    return cnt


SAMPLE_CHUNK = 512
M_FLOOR = -1e30


def _attn_sample_kernel(q_ref, kn_ref, vn_ref, kn3_ref, vn3_ref, ks_ref, ks_next_ref, vs_ref, vs_next_ref,
                        o_ref, ko_ref, vo_ref, m_sc, l_sc, acc_sc, *, n_buf, chunks):
    c = pl.program_id(1)
    rows = ks_ref.shape[0]
    t_new = q_ref.shape[0]
    scale = HEAD_DIM ** -0.5
    log_heads = int(math.log2(ATT_HEADS))
    log_new = int(math.log2(t_new))
    assert ATT_HEADS == 1 << log_heads and t_new == 1 << log_new

    def by_head(ref):
        return jnp.concatenate([ref[:, h * HEAD_DIM:(h + 1) * HEAD_DIM] for h in range(ATT_HEADS)], axis=0)

    @pl.when(c == 0)
    def _():
        m_sc[...] = jnp.full(m_sc.shape, M_FLOOR, F32)
        l_sc[...] = jnp.zeros(l_sc.shape, F32)
        acc_sc[...] = jnp.zeros(acc_sc.shape, F32)

    q = by_head(q_ref).astype(BF16)

    def accumulate(k2, v2, key_head, rel):
        q_head = lax.shift_right_logical(lax.broadcasted_iota(I32, rel.shape, 0), log_new)
        cnt = jnp.where(key_head == q_head, _branch_count(rel), 0.0)
        s = jnp.where(cnt > 0, _dot_t(q, k2.astype(BF16)) * scale, NEG_INF)
        m_old = m_sc[...]
        m_new = jnp.maximum(m_old, jnp.max(s, axis=1, keepdims=True))
        alpha = jnp.exp(m_old - m_new)
        p = cnt * jnp.exp(s - m_new)
        l_sc[...] = alpha * l_sc[...] + jnp.sum(p, axis=1, keepdims=True)
        acc_sc[...] = alpha * acc_sc[...] + jnp.dot(p.astype(BF16), v2.astype(BF16),
                                                    preferred_element_type=F32)
        m_sc[...] = m_new

    ks = ks_ref[...]
    vs = vs_ref[...]
    n = rows * ATT_HEADS
    col = lax.broadcasted_iota(I32, (ATT_HEADS * t_new, n), 1)
    qrow = lax.broadcasted_iota(I32, (ATT_HEADS * t_new, n), 0)
    kpos = c * rows + lax.shift_right_logical(col, log_heads)
    qpos = n_buf + (qrow & (t_new - 1))
    accumulate(ks.reshape(n, HEAD_DIM), vs.reshape(n, HEAD_DIM), col & (ATT_HEADS - 1), qpos - kpos)

    last = c == chunks - 1
    ko_ref[0:rows - t_new] = ks[t_new:]
    vo_ref[0:rows - t_new] = vs[t_new:]
    ko_ref[rows - t_new:rows] = jnp.where(last, kn3_ref[...], ks_next_ref[...])
    vo_ref[rows - t_new:rows] = jnp.where(last, vn3_ref[...], vs_next_ref[...])

    @pl.when(last)
    def _():
        m2 = ATT_HEADS * t_new
        ncol = lax.broadcasted_iota(I32, (m2, m2), 1)
        nrow = lax.broadcasted_iota(I32, (m2, m2), 0)
        accumulate(by_head(kn_ref), by_head(vn_ref), lax.shift_right_logical(ncol, log_new),
                   (nrow & (t_new - 1)) - (ncol & (t_new - 1)))
        out = acc_sc[...] / l_sc[...]
        for h in range(ATT_HEADS):
            o_ref[:, h * HEAD_DIM:(h + 1) * HEAD_DIM] = out[h * t_new:(h + 1) * t_new, :]


def _attn_sample(q, k_new, v_new, k_state, v_state, batch, t_new):
    n_buf = k_state.shape[2]
    rows = min(SAMPLE_CHUNK, n_buf)
    chunks = n_buf // rows
    assert t_new == SUBLANES and rows % t_new == 0
    kn3 = k_new.reshape(batch * t_new, ATT_HEADS, HEAD_DIM)
    vn3 = v_new.reshape(batch * t_new, ATT_HEADS, HEAD_DIM)
    new = pl.BlockSpec((t_new, D_ATT), lambda b, c: (b, 0))
    new3 = pl.BlockSpec((t_new, ATT_HEADS, HEAD_DIM), lambda b, c: (b, 0, 0))
    buf = pl.BlockSpec((None, None, rows, ATT_HEADS, HEAD_DIM), lambda b, c: (0, b, c, 0, 0))
    per = rows // t_new
    nxt = pl.BlockSpec((None, None, t_new, ATT_HEADS, HEAD_DIM),
                       lambda b, c: (0, b, jnp.minimum((c + 1) * per, n_buf // t_new - 1), 0, 0))
    buf_shape = jax.ShapeDtypeStruct(k_state.shape, F32)
    qrows = ATT_HEADS * t_new
    return pl.pallas_call(
        functools.partial(_attn_sample_kernel, n_buf=n_buf, chunks=chunks),
        grid=(batch, chunks),
        in_specs=[new, new, new, new3, new3, buf, nxt, buf, nxt],
        out_specs=[new, buf, buf],
        out_shape=[jax.ShapeDtypeStruct((batch * t_new, D_ATT), F32), buf_shape, buf_shape],
        scratch_shapes=[pltpu.VMEM((qrows, 1), F32), pltpu.VMEM((qrows, 1), F32),
                        pltpu.VMEM((qrows, HEAD_DIM), F32)],
        compiler_params=_params("parallel", "arbitrary"),
        name="attn_sample",
    )(q, k_new, v_new, kn3, vn3, k_state, k_state, v_state, v_state)


def _out_proj_kernel(att_ref, conv_ref, x_ref, w_ref, g_ref, b_ref, o_ref):
    mixed = (jnp.dot(att_ref[...].astype(BF16), w_ref[0:D_ATT, :], preferred_element_type=F32)
             + jnp.dot(conv_ref[...].astype(BF16), w_ref[D_ATT:D_MODEL, :], preferred_element_type=F32))
    o_ref[...] = _layer_norm(DEEPNORM_ALPHA * x_ref[...] + mixed, g_ref[...], b_ref[...])


def _out_proj(att, conv, x, w_bf16, g, b):
    n = x.shape[0]
    tm = min(ROW_TILE, n)
    row = lambda i: (i, 0)
    return pl.pallas_call(
        _out_proj_kernel,
        grid=(n // tm,),
        in_specs=[pl.BlockSpec((tm, D_ATT), row), pl.BlockSpec((tm, D_CONV), row),
                  pl.BlockSpec((tm, D_MODEL), row), _resident(w_bf16.shape),
                  _resident(g.shape), _resident(b.shape)],
        out_specs=pl.BlockSpec((tm, D_MODEL), row),
        out_shape=jax.ShapeDtypeStruct((n, D_MODEL), F32),
        compiler_params=_params("parallel"),
        name="out_proj",
    )(att, conv, x, w_bf16, g, b)


ID_NONE = float(2 ** 24)


def _top_rows(s, ids, k):
    vals, sel = [], []
    for _ in range(k):
        m = jnp.max(s, axis=0, keepdims=True)
        i = jnp.min(jnp.where(s == m, ids, ID_NONE), axis=0, keepdims=True)
        vals.append(m)
        sel.append(i)
        s = jnp.where(ids == i, NEG_INF, s)
    return jnp.concatenate(vals, axis=0), jnp.concatenate(sel, axis=0)


def _pair_candidates(v1, v2, sub):
    assert PEER_TOPK == 2 * SUBLANES
    vals = [v1[0:1, :] + v2]
    ids = [sub, sub + SUBLANES]
    for i in range(1, SUBLANES):
        vals.append(v1[i:i + 1, :] + v2[0:SUBLANES, :])
        ids.append(sub + i * PEER_TOPK)
    vals.append(v1[SUBLANES:, :] + v2[0:1, :])
    ids.append((sub + SUBLANES) * PEER_TOPK)
    return jnp.concatenate(vals, axis=0), jnp.concatenate(ids, axis=0)


def _take_rows(table, idx):
    out = jnp.zeros(idx.shape, table.dtype)
    for i in range(table.shape[0]):
        out = jnp.where(idx == i, table[i:i + 1, :], out)
    return out


def _peer_route_kernel(h_ref, wq_ref, sk_ref, e_ref, g_ref):
    hb = h_ref[...].astype(BF16)
    tokens = hb.shape[0]
    half = PEER_KEY_DIM // 2
    key_ids = lax.broadcasted_iota(I32, (N_SUBKEYS, tokens), 0).astype(F32)
    sub = lax.broadcasted_iota(I32, (SUBLANES, tokens), 0).astype(F32)
    experts, gates = [], []
    for head in range(PEER_HEADS):
        top = []
        for p in range(2):
            c0 = head * PEER_KEY_DIM + p * half
            qhp = jnp.dot(hb, wq_ref[:, c0:c0 + half], preferred_element_type=F32)
            sc = _dot_t(sk_ref[p], qhp.astype(BF16))
            top.append(_top_rows(sc, key_ids, PEER_TOPK))
        (v1, i1), (v2, i2) = top
        cand, cand_ids = _pair_candidates(v1, v2, sub)
        c_top, c_id = _top_rows(cand, cand_ids, PEER_TOPK)
        rank1 = jnp.floor(c_id * (1.0 / PEER_TOPK))
        rank2 = c_id - rank1 * PEER_TOPK
        experts.append(_take_rows(i1, rank1) * N_SUBKEYS + _take_rows(i2, rank2))
        ex = jnp.exp(c_top - jnp.max(c_top, axis=0, keepdims=True))
        gates.append(ex / jnp.sum(ex, axis=0, keepdims=True))
    e_ref[...] = jnp.concatenate(experts, axis=0).T.astype(I32)
    g_ref[...] = jnp.concatenate(gates, axis=0).T


def _peer_route(h, wq_bf16, subkeys_bf16):
    n = h.shape[0]
    tm = min(ROW_TILE, n)
    row = lambda i: (i, 0)
    return pl.pallas_call(
        _peer_route_kernel,
        grid=(n // tm,),
        in_specs=[pl.BlockSpec((tm, D_MODEL), row), _resident(wq_bf16.shape),
                  _resident(subkeys_bf16.shape)],
        out_specs=[pl.BlockSpec((tm, PEER_SLOTS), row)] * 2,
        out_shape=[jax.ShapeDtypeStruct((n, PEER_SLOTS), I32),
                   jax.ShapeDtypeStruct((n, PEER_SLOTS), F32)],
        compiler_params=_params("parallel"),
        name="peer_route",
    )(h, wq_bf16, subkeys_bf16)


PEER_TOKENS = 64
PEER_RING = 8


def _fold(row):
    return jnp.concatenate([row[:, j * LANES:(j + 1) * LANES] for j in range(FOLD)], axis=0)


def _unfold(folded):
    return jnp.concatenate([folded[j:j + 1, :] for j in range(FOLD)], axis=1)


def _sublane_sums(vregs):
    sub = lax.broadcasted_iota(I32, (SUBLANES, LANES), 0)
    roll = lambda a, k: pltpu.roll(a, k, axis=0)
    p = [vregs[i] for i in (3, 2, 1, 0, 7, 6, 5, 4)]
    m = [jnp.where(sub < 4, p[i] + roll(p[i], 4), p[i + 4] + roll(p[i + 4], 4)) for i in range(4)]
    n = [jnp.where((sub & 3) >= 2, m[i] + roll(m[i], 2), roll(m[i + 2] + roll(m[i + 2], 2), 6))
         for i in range(2)]
    return jnp.where((sub & 1) == 1, n[0] + roll(n[0], 1), roll(n[1] + roll(n[1], 1), 7))


def _peer_expert_kernel(e_ref, e_next_ref, h_ref, g_ref, uv_hbm, o_ref, buf, sem, *, steps):
    tokens = h_ref.shape[0]
    ahead = PEER_RING - 1
    assert tokens % PEER_RING == 0 and tokens > ahead and FOLD == 2 * SUBLANES
    step = pl.program_id(0)
    eye = (lax.broadcasted_iota(I32, (PEER_SLOTS, PEER_SLOTS), 0)
           == lax.broadcasted_iota(I32, (PEER_SLOTS, PEER_SLOTS), 1))

    pieces = 2 * PEER_SLOTS // SUBLANES
    per_piece = PEER_SLOTS // pieces

    def start_piece(ids_ref, t, slot, piece):
        for k in range(piece * per_piece, (piece + 1) * per_piece):
            pltpu.make_async_copy(uv_hbm.at[ids_ref[t, k]], buf.at[slot, k], sem.at[slot]).start(priority=k % 2)

    def wait(slot):
        pltpu.make_async_copy(uv_hbm.at[pl.ds(0, PEER_SLOTS)], buf.at[slot], sem.at[slot]).wait()

    def token(ids_ref, t_ahead, slot_ahead, t, slot):
        wait(slot)
        x = _fold(h_ref[pl.ds(t, 1), :])
        sums = []
        for c in range(pieces // 2):
            start_piece(ids_ref, t_ahead, slot_ahead, c)
            rows = slice(c * SUBLANES, (c + 1) * SUBLANES)
            prod = buf[slot, rows, 0:FOLD, :].astype(F32) * x[None]
            sums.append(_sublane_sums([prod[i, 0:SUBLANES, :] + prod[i, SUBLANES:FOLD, :]
                                       for i in range(SUBLANES)]))
        hid = jnp.sum(jnp.concatenate(sums, axis=0), axis=1, keepdims=True)
        gate = jnp.sum(jnp.where(eye, g_ref[pl.ds(t, 1), :], 0.0), axis=1, keepdims=True)
        a = gate * (0.5 * hid * (1.0 + lax.erf(hid * (2.0 ** -0.5))))
        a = jnp.broadcast_to(a, (PEER_SLOTS, LANES))
        y = jnp.zeros((FOLD, LANES), F32)
        for c in range(pieces // 2):
            start_piece(ids_ref, t_ahead, slot_ahead, pieces // 2 + c)
            rows = slice(c * SUBLANES, (c + 1) * SUBLANES)
            y = y + jnp.sum(a[rows][:, None, :] * buf[slot, rows, FOLD:2 * FOLD, :].astype(F32), axis=0)
        o_ref[pl.ds(t, 1), :] = _unfold(y)

    @pl.when(step == 0)
    def _():
        for t in range(ahead):
            for piece in range(pieces):
                start_piece(e_ref, t, t, piece)

    def body(t, carry):
        token(e_ref, t + ahead, (t + ahead) % PEER_RING, t, t % PEER_RING)
        return carry

    lax.fori_loop(0, tokens - ahead, body, 0)
    for j in range(ahead):
        t = tokens - ahead + j
        token(e_next_ref, j, j % PEER_RING, t, t % PEER_RING)

    @pl.when(step == steps - 1)
    def _():
        for j in range(ahead):
            wait(j % PEER_RING)


def _peer_expert(experts, h, gates, uv_tab):
    n = h.shape[0]
    tb = min(PEER_TOKENS, n)
    steps = n // tb
    row = lambda i: (i, 0)
    return pl.pallas_call(
        functools.partial(_peer_expert_kernel, steps=steps),
        grid=(steps,),
        in_specs=[pl.BlockSpec((tb, PEER_SLOTS), row, memory_space=pltpu.SMEM),
                  pl.BlockSpec((tb, PEER_SLOTS), lambda i: (jnp.minimum(i + 1, steps - 1), 0),
                               memory_space=pltpu.SMEM),
                  pl.BlockSpec((tb, D_MODEL), row),
                  pl.BlockSpec((tb, PEER_SLOTS), row),
                  pl.BlockSpec(memory_space=pl.ANY)],
        out_specs=pl.BlockSpec((tb, D_MODEL), row),
        out_shape=jax.ShapeDtypeStruct((n, D_MODEL), F32),
        scratch_shapes=[pltpu.VMEM((PEER_RING, PEER_SLOTS, 2 * FOLD, LANES), uv_tab.dtype),
                        pltpu.SemaphoreType.DMA((PEER_RING,))],
        compiler_params=_params("arbitrary"),
        name="peer_expert",
    )(experts, experts, h, gates, uv_tab)


def _tail_kernel(h_ref, y_ref, p_ref, g_ref, b_ref, wg_ref, wp_ref, o_ref):
    h2 = _layer_norm(DEEPNORM_ALPHA * h_ref[...] + y_ref[...], g_ref[...], b_ref[...])
    gate = jax.nn.sigmoid(jnp.dot(h2.astype(BF16), wg_ref[...], preferred_element_type=F32))
    emb = jnp.dot(p_ref[...].astype(BF16), wp_ref[...], preferred_element_type=F32)
    o_ref[...] = h2 + gate * emb


def _tail(h, y, p, g, b, wg_bf16, wp_bf16):
    n = h.shape[0]
    tm = min(ROW_TILE, n)
    row = lambda i: (i, 0)
    return pl.pallas_call(
        _tail_kernel,
        grid=(n // tm,),
        in_specs=[pl.BlockSpec((tm, D_MODEL), row), pl.BlockSpec((tm, D_MODEL), row),
                  pl.BlockSpec((tm, p.shape[1]), row), _resident(g.shape), _resident(b.shape),
                  _resident(wg_bf16.shape), _resident(wp_bf16.shape)],
        out_specs=pl.BlockSpec((tm, D_MODEL), row),
        out_shape=jax.ShapeDtypeStruct((n, D_MODEL), F32),
        compiler_params=_params("parallel"),
        name="tail",
    )(h, y, p, g, b, wg_bf16, wp_bf16)


def _rotary_tables(pos):
    half = HEAD_DIM // 2
    inv = ROPE_THETA ** (-jnp.arange(half, dtype=F32) / half)
    ang = pos.astype(F32)[:, None] * inv[None, :]
    cos, sin = jnp.cos(ang), jnp.sin(ang)
    return jnp.concatenate([cos, cos], axis=1), jnp.concatenate([-sin, sin], axis=1)


def _state_halo(state):
    return jnp.pad(state, ((0, 0), (SUBLANES - (CONV_WIDTH - 1), 0), (0, 0)))


def _layer_tail(x, att, conv, p, w):
    h1 = _out_proj(att, conv, x, w["w_out"], w["ln1_g"], w["ln1_b"])
    experts, gates = _peer_route(h1, w["peer_wq"], w["peer_subkeys"])
    y = _peer_expert(experts, h1, gates, w["peer_uv"])
    return _tail(h1, y, p, w["ln2_g"], w["ln2_b"], w["ple_gate"], w["ple_proj"])


def kernel(x_prompt, x_sample, state_attn_k, state_attn_v, state_conv, p_prompt, p_sample,
           w_in, w_conv, w_out, ln1_g, ln1_b, peer_wq, peer_subkeys, peer_u, peer_v,
           ln2_g, ln2_b, ple_gate, ple_proj):
    assert w_in.shape[0] == DEPTH
    bp, tp, _ = x_prompt.shape
    bs, ts, _ = x_sample.shape
    w = {
        "w_in": w_in[0].astype(BF16), "w_out": w_out[0].astype(BF16),
        "ln1_g": ln1_g, "ln1_b": ln1_b, "ln2_g": ln2_g, "ln2_b": ln2_b,
        "peer_wq": peer_wq[0].astype(BF16), "peer_subkeys": peer_subkeys[0].astype(BF16),
        "peer_uv": jnp.concatenate([peer_u.reshape(-1, FOLD, LANES), peer_v.reshape(-1, FOLD, LANES)],
                                   axis=1).astype(BF16),
        "ple_gate": ple_gate[0].astype(BF16), "ple_proj": ple_proj[0].astype(BF16),
    }
    wc = w_conv[0]

    xp = x_prompt.reshape(bp * tp, D_MODEL)
    cos_p, sin_p = _rotary_tables(jnp.arange(tp))
    q, k, v, z, gb = _in_proj(xp, w["w_in"], cos_p, sin_p)
    conv = _conv(z, gb, jnp.zeros((bp, SUBLANES, D_CONV), F32), wc, bp, tp)
    att = _attn_prompt(q, k, v, bp, tp)
    y_prompt = _layer_tail(xp, att, conv, p_prompt[0].reshape(bp * tp, -1), w).reshape(bp, tp, D_MODEL)
    new_k_prompt = k.reshape(1, bp, tp, ATT_HEADS, HEAD_DIM)
    new_v_prompt = v.reshape(1, bp, tp, ATT_HEADS, HEAD_DIM)
    new_conv_prompt = z.reshape(bp, tp, D_CONV)[:, tp - (CONV_WIDTH - 1):][None]

    xs = x_sample.reshape(bs * ts, D_MODEL)
    cos_s, sin_s = _rotary_tables(jnp.tile(PAST_LEN + jnp.arange(ts), bs))
    q, k, v, z, gb = _in_proj(xs, w["w_in"], cos_s, sin_s)
    conv = _conv(z, gb, _state_halo(state_conv[0]), wc, bs, ts)
    att, new_k_sample, new_v_sample = _attn_sample(q, k, v, state_attn_k, state_attn_v, bs, ts)
    y_sample = _layer_tail(xs, att, conv, p_sample[0].reshape(bs * ts, -1), w).reshape(bs, ts, D_MODEL)
    z_ext = jnp.concatenate([state_conv[0], z.reshape(bs, ts, D_CONV)], axis=1)
    new_conv_sample = z_ext[:, -(CONV_WIDTH - 1):][None]

    return (y_prompt, y_sample, new_k_prompt, new_v_prompt, new_conv_prompt,
            new_k_sample, new_v_sample, new_conv_sample)
```

```python
import functools
import math

import jax
import jax.numpy as jnp
from jax import lax
from jax.experimental import pallas as pl
from jax.experimental.pallas import tpu as pltpu

F32 = jnp.float32
BF16 = jnp.bfloat16
I32 = jnp.int32

D_MODEL = 2048
PAST_LEN = 16384
ATT_HEADS = 8
HEAD_DIM = 128
D_ATT = ATT_HEADS * HEAD_DIM
D_CONV = D_MODEL - D_ATT
CONV_WIDTH = 3
DILATIONS = (1, 4, 16)
BAND = 128
ROPE_THETA = 10000.0
PEER_HEADS = 8
N_SUBKEYS = 128
PEER_KEY_DIM = 256
PEER_TOPK = 16
PEER_SLOTS = PEER_HEADS * PEER_TOPK
DEPTH = 1
DEEPNORM_ALPHA = (2.0 * DEPTH) ** 0.25
LN_EPS = 1e-5

LANES = 128
SUBLANES = 8
FOLD = D_MODEL // LANES
ROW_TILE = 256
VMEM_LIMIT = 56 * 1024 * 1024
NEG_INF = float("-inf")


def _params(*sem):
    return pltpu.CompilerParams(dimension_semantics=sem, vmem_limit_bytes=VMEM_LIMIT)


def _resident(shape):
    nd = len(shape)
    return pl.BlockSpec(shape, lambda *_: (0,) * nd, pipeline_mode=pl.Buffered(1))


def _layer_norm(y, g, b):
    mu = jnp.mean(y, axis=-1, keepdims=True)
    c = y - mu
    var = jnp.mean(c * c, axis=-1, keepdims=True)
    return c * lax.rsqrt(var + LN_EPS) * g + b


def _in_proj_kernel(x_ref, w_ref, cos_ref, sin_ref, q_ref, k_ref, v_ref, z_ref, gb_ref):
    xb = x_ref[...].astype(BF16)
    cos = cos_ref[...]
    sin = sin_ref[...]

    def proj(c):
        return jnp.dot(xb, w_ref[:, c * D_ATT:(c + 1) * D_ATT], preferred_element_type=F32)

    def store_rotary(a, o_ref):
        for h in range(ATT_HEADS):
            ah = a[:, h * HEAD_DIM:(h + 1) * HEAD_DIM]
            o_ref[:, h * HEAD_DIM:(h + 1) * HEAD_DIM] = (
                ah * cos + pltpu.roll(ah, HEAD_DIM // 2, axis=1) * sin)

    store_rotary(proj(0), q_ref)
    store_rotary(proj(1), k_ref)
    v_ref[...] = proj(2)
    u = proj(3)
    gb_ref[...] = proj(4)
    z_ref[...] = proj(5) * u


def _in_proj(x, w_bf16, cos_tab, sin_tab):
    n = x.shape[0]
    tm = min(ROW_TILE, n)
    n_tab = cos_tab.shape[0] // tm
    row = lambda i: (i, 0)
    out = jax.ShapeDtypeStruct((n, D_ATT), F32)
    return pl.pallas_call(
        _in_proj_kernel,
        grid=(n // tm,),
        in_specs=[pl.BlockSpec((tm, D_MODEL), row),
                  _resident(w_bf16.shape),
                  pl.BlockSpec((tm, HEAD_DIM), lambda i: (i % n_tab, 0)),
                  pl.BlockSpec((tm, HEAD_DIM), lambda i: (i % n_tab, 0))],
        out_specs=[pl.BlockSpec((tm, D_ATT), row)] * 5,
        out_shape=[out] * 5,
        compiler_params=_params("parallel"),
        name="in_proj",
    )(x, w_bf16, cos_tab, sin_tab)


def _conv_kernel(z_ref, zprev_ref, st_ref, gb_ref, wc_ref, o_ref):
    halo = jnp.where(pl.program_id(1) == 0, st_ref[...], zprev_ref[...])
    z = z_ref[...]
    row = lax.broadcasted_iota(I32, z.shape, 0)
    h1 = halo[SUBLANES - 1:SUBLANES, :]
    h2 = halo[SUBLANES - 2:SUBLANES - 1, :]
    zm1 = jnp.where(row == 0, h1, pltpu.roll(z, 1, axis=0))
    zm2 = jnp.where(row == 0, h2, jnp.where(row == 1, h1, pltpu.roll(z, 2, axis=0)))
    wc = wc_ref[...]
    o_ref[...] = gb_ref[...] * (wc[0:1, :] * zm2 + wc[1:2, :] * zm1 + wc[2:3, :] * z)


def _conv(z, gb, state_halo, w_conv, batch, seq):
    tt = min(ROW_TILE, seq)
    per_seq = seq // tt
    halo_per_tile = tt // SUBLANES
    z3 = z.reshape(batch, seq, D_CONV)
    gb3 = gb.reshape(batch, seq, D_CONV)
    tile = pl.BlockSpec((None, tt, D_CONV), lambda b, i: (b, i, 0))
    out = pl.pallas_call(
        _conv_kernel,
        grid=(batch, per_seq),
        in_specs=[tile,
                  pl.BlockSpec((None, SUBLANES, D_CONV),
                               lambda b, i: (b, jnp.maximum(i * halo_per_tile - 1, 0), 0)),
                  pl.BlockSpec((None, SUBLANES, D_CONV), lambda b, i: (b, 0, 0)),
                  tile,
                  pl.BlockSpec((CONV_WIDTH, D_CONV), lambda b, i: (0, 0))],
        out_specs=tile,
        out_shape=jax.ShapeDtypeStruct((batch, seq, D_CONV), F32),
        compiler_params=_params("parallel", "parallel"),
        name="conv",
    )(z3, z3, state_halo, gb3, w_conv)
    return out.reshape(batch * seq, D_CONV)


def _dot_t(a, b):
    return lax.dot_general(a, b, (((1,), (1,)), ((), ())), preferred_element_type=F32)


ATTN_UNROLL = 4


def _attn_prompt_kernel(q_ref, k_ref, v_ref, o_ref, acc_sc, m_sc, l_sc):
    seq = q_ref.shape[0]
    scale = HEAD_DIM ** -0.5
    qi = lax.broadcasted_iota(I32, (BAND, 2 * BAND), 0)
    kj = lax.broadcasted_iota(I32, (BAND, 2 * BAND), 1)
    band_ok = (kj >= qi) & (kj <= qi + BAND)
    is_prev = kj < BAND
    own_ok = (lax.broadcasted_iota(I32, (BAND, BAND), 1) <= lax.broadcasted_iota(I32, (BAND, BAND), 0))

    for bi, d in enumerate(DILATIONS):
        nb = seq // (d * BAND)

        def block(blk, carry, bi=bi, d=d, nb=nb):
            r = blk // nb
            n = blk % nb
            start = r + n * (d * BAND)
            rows = pl.ds(start, BAND, stride=d) if d > 1 else pl.ds(pl.multiple_of(start, BAND), BAND)
            q = q_ref[rows, :].astype(BF16)
            if nb > 1:
                prev_start = jnp.maximum(start - d * BAND, 0)
                prows = (pl.ds(prev_start, BAND, stride=d) if d > 1
                         else pl.ds(pl.multiple_of(prev_start, BAND), BAND))
                keys = jnp.concatenate([k_ref[prows, :], k_ref[rows, :]], axis=0).astype(BF16)
                vals = jnp.concatenate([v_ref[prows, :], v_ref[rows, :]], axis=0)
                s = (jnp.where(band_ok, _dot_t(q, keys) * scale, NEG_INF)
                     + jnp.where(is_prev, jnp.where(n > 0, 0.0, NEG_INF), 0.0))
            else:
                keys = k_ref[rows, :].astype(BF16)
                vals = v_ref[rows, :]
                s = jnp.where(own_ok, _dot_t(q, keys) * scale, NEG_INF)
            m = jnp.max(s, axis=1, keepdims=True)
            p = jnp.exp(s - m).astype(BF16)
            vals_ones = jnp.concatenate([vals, jnp.ones(vals.shape, F32)], axis=1).astype(BF16)
            ext = jnp.dot(p, vals_ones, preferred_element_type=F32)
            acc_sc[bi, rows, :] = ext[:, :HEAD_DIM]
            l_sc[bi, rows, :] = ext[:, HEAD_DIM:]
            m_sc[bi, rows, :] = jnp.broadcast_to(m, (BAND, HEAD_DIM))
            return carry

        lax.fori_loop(0, d * nb, block, 0, unroll=ATTN_UNROLL)

    big_m = jnp.maximum(jnp.maximum(m_sc[0], m_sc[1]), m_sc[2])
    num = jnp.zeros((seq, HEAD_DIM), F32)
    den = jnp.zeros((seq, HEAD_DIM), F32)
    for bi in range(len(DILATIONS)):
        w = jnp.exp(m_sc[bi] - big_m)
        num = num + w * acc_sc[bi]
        den = den + w * l_sc[bi]
    o_ref[...] = num / den


def _attn_prompt(q, k, v, batch, seq):
    q3, k3, v3 = (a.reshape(batch, seq, D_ATT) for a in (q, k, v))
    head = pl.BlockSpec((None, seq, HEAD_DIM), lambda b, h: (b, 0, h))
    nbr = len(DILATIONS)
    out = pl.pallas_call(
        _attn_prompt_kernel,
        grid=(batch, ATT_HEADS),
        in_specs=[head, head, head],
        out_specs=head,
        out_shape=jax.ShapeDtypeStruct((batch, seq, D_ATT), F32),
        scratch_shapes=[pltpu.VMEM((nbr, seq, HEAD_DIM), F32)] * 3,
        compiler_params=_params("parallel", "parallel"),
        name="attn_prompt",
    )(q3, k3, v3)
    return out.reshape(batch * seq, D_ATT)


def _branch_count(rel):
    cnt = jnp.zeros(rel.shape, F32)
    for d in DILATIONS:
        hit = (rel >= 0) & (rel <= d * BAND) & ((rel & (d - 1)) == 0)
        cnt = cnt + jnp.where(hit, 1.0, 0.0)
    return cnt


SAMPLE_CHUNK = 512
M_FLOOR = -1e30


def _attn_sample_kernel(q_ref, kn_ref, vn_ref, kn3_ref, vn3_ref, ks_ref, ks_next_ref, vs_ref, vs_next_ref,
                        o_ref, ko_ref, vo_ref, m_sc, l_sc, acc_sc, *, n_buf, chunks):
    c = pl.program_id(1)
    rows = ks_ref.shape[0]
    t_new = q_ref.shape[0]
    scale = HEAD_DIM ** -0.5
    log_heads = int(math.log2(ATT_HEADS))
    log_new = int(math.log2(t_new))
    assert ATT_HEADS == 1 << log_heads and t_new == 1 << log_new

    def by_head(ref):
        return jnp.concatenate([ref[:, h * HEAD_DIM:(h + 1) * HEAD_DIM] for h in range(ATT_HEADS)], axis=0)

    @pl.when(c == 0)
    def _():
        m_sc[...] = jnp.full(m_sc.shape, M_FLOOR, F32)
        l_sc[...] = jnp.zeros(l_sc.shape, F32)
        acc_sc[...] = jnp.zeros(acc_sc.shape, F32)

    q = by_head(q_ref).astype(BF16)

    def accumulate(k2, v2, key_head, rel):
        q_head = lax.shift_right_logical(lax.broadcasted_iota(I32, rel.shape, 0), log_new)
        cnt = jnp.where(key_head == q_head, _branch_count(rel), 0.0)
        s = jnp.where(cnt > 0, _dot_t(q, k2.astype(BF16)) * scale, NEG_INF)
        m_old = m_sc[...]
        m_new = jnp.maximum(m_old, jnp.max(s, axis=1, keepdims=True))
        alpha = jnp.exp(m_old - m_new)
        p = cnt * jnp.exp(s - m_new)
        l_sc[...] = alpha * l_sc[...] + jnp.sum(p, axis=1, keepdims=True)
        acc_sc[...] = alpha * acc_sc[...] + jnp.dot(p.astype(BF16), v2.astype(BF16),
                                                    preferred_element_type=F32)
        m_sc[...] = m_new

    ks = ks_ref[...]
    vs = vs_ref[...]
    n = rows * ATT_HEADS
    col = lax.broadcasted_iota(I32, (ATT_HEADS * t_new, n), 1)
    qrow = lax.broadcasted_iota(I32, (ATT_HEADS * t_new, n), 0)
    kpos = c * rows + lax.shift_right_logical(col, log_heads)
    qpos = n_buf + (qrow & (t_new - 1))
    accumulate(ks.reshape(n, HEAD_DIM), vs.reshape(n, HEAD_DIM), col & (ATT_HEADS - 1), qpos - kpos)

    last = c == chunks - 1
    ko_ref[0:rows - t_new] = ks[t_new:]
    vo_ref[0:rows - t_new] = vs[t_new:]
    ko_ref[rows - t_new:rows] = jnp.where(last, kn3_ref[...], ks_next_ref[...])
    vo_ref[rows - t_new:rows] = jnp.where(last, vn3_ref[...], vs_next_ref[...])

    @pl.when(last)
    def _():
        m2 = ATT_HEADS * t_new
        ncol = lax.broadcasted_iota(I32, (m2, m2), 1)
        nrow = lax.broadcasted_iota(I32, (m2, m2), 0)
        accumulate(by_head(kn_ref), by_head(vn_ref), lax.shift_right_logical(ncol, log_new),
                   (nrow & (t_new - 1)) - (ncol & (t_new - 1)))
        out = acc_sc[...] / l_sc[...]
        for h in range(ATT_HEADS):
            o_ref[:, h * HEAD_DIM:(h + 1) * HEAD_DIM] = out[h * t_new:(h + 1) * t_new, :]


def _attn_sample(q, k_new, v_new, k_state, v_state, batch, t_new):
    n_buf = k_state.shape[2]
    rows = min(SAMPLE_CHUNK, n_buf)
    chunks = n_buf // rows
    assert t_new == SUBLANES and rows % t_new == 0
    kn3 = k_new.reshape(batch * t_new, ATT_HEADS, HEAD_DIM)
    vn3 = v_new.reshape(batch * t_new, ATT_HEADS, HEAD_DIM)
    new = pl.BlockSpec((t_new, D_ATT), lambda b, c: (b, 0))
    new3 = pl.BlockSpec((t_new, ATT_HEADS, HEAD_DIM), lambda b, c: (b, 0, 0))
    buf = pl.BlockSpec((None, None, rows, ATT_HEADS, HEAD_DIM), lambda b, c: (0, b, c, 0, 0))
    per = rows // t_new
    nxt = pl.BlockSpec((None, None, t_new, ATT_HEADS, HEAD_DIM),
                       lambda b, c: (0, b, jnp.minimum((c + 1) * per, n_buf // t_new - 1), 0, 0))
    buf_shape = jax.ShapeDtypeStruct(k_state.shape, F32)
    qrows = ATT_HEADS * t_new
    return pl.pallas_call(
        functools.partial(_attn_sample_kernel, n_buf=n_buf, chunks=chunks),
        grid=(batch, chunks),
        in_specs=[new, new, new, new3, new3, buf, nxt, buf, nxt],
        out_specs=[new, buf, buf],
        out_shape=[jax.ShapeDtypeStruct((batch * t_new, D_ATT), F32), buf_shape, buf_shape],
        scratch_shapes=[pltpu.VMEM((qrows, 1), F32), pltpu.VMEM((qrows, 1), F32),
                        pltpu.VMEM((qrows, HEAD_DIM), F32)],
        compiler_params=_params("parallel", "arbitrary"),
        name="attn_sample",
    )(q, k_new, v_new, kn3, vn3, k_state, k_state, v_state, v_state)


def _out_proj_kernel(att_ref, conv_ref, x_ref, w_ref, g_ref, b_ref, o_ref):
    mixed = (jnp.dot(att_ref[...].astype(BF16), w_ref[0:D_ATT, :], preferred_element_type=F32)
             + jnp.dot(conv_ref[...].astype(BF16), w_ref[D_ATT:D_MODEL, :], preferred_element_type=F32))
    o_ref[...] = _layer_norm(DEEPNORM_ALPHA * x_ref[...] + mixed, g_ref[...], b_ref[...])


def _out_proj(att, conv, x, w_bf16, g, b):
    n = x.shape[0]
    tm = min(ROW_TILE, n)
    row = lambda i: (i, 0)
    return pl.pallas_call(
        _out_proj_kernel,
        grid=(n // tm,),
        in_specs=[pl.BlockSpec((tm, D_ATT), row), pl.BlockSpec((tm, D_CONV), row),
                  pl.BlockSpec((tm, D_MODEL), row), _resident(w_bf16.shape),
                  _resident(g.shape), _resident(b.shape)],
        out_specs=pl.BlockSpec((tm, D_MODEL), row),
        out_shape=jax.ShapeDtypeStruct((n, D_MODEL), F32),
        compiler_params=_params("parallel"),
        name="out_proj",
    )(att, conv, x, w_bf16, g, b)


ID_NONE = float(2 ** 24)


def _top_rows(s, ids, k):
    vals, sel = [], []
    for _ in range(k):
        m = jnp.max(s, axis=0, keepdims=True)
        i = jnp.min(jnp.where(s == m, ids, ID_NONE), axis=0, keepdims=True)
        vals.append(m)
        sel.append(i)
        s = jnp.where(ids == i, NEG_INF, s)
    return jnp.concatenate(vals, axis=0), jnp.concatenate(sel, axis=0)


def _pair_candidates(v1, v2, sub):
    assert PEER_TOPK == 2 * SUBLANES
    vals = [v1[0:1, :] + v2]
    ids = [sub, sub + SUBLANES]
    for i in range(1, SUBLANES):
        vals.append(v1[i:i + 1, :] + v2[0:SUBLANES, :])
        ids.append(sub + i * PEER_TOPK)
    vals.append(v1[SUBLANES:, :] + v2[0:1, :])
    ids.append((sub + SUBLANES) * PEER_TOPK)
    return jnp.concatenate(vals, axis=0), jnp.concatenate(ids, axis=0)


def _take_rows(table, idx):
    out = jnp.zeros(idx.shape, table.dtype)
    for i in range(table.shape[0]):
        out = jnp.where(idx == i, table[i:i + 1, :], out)
    return out


def _peer_route_kernel(h_ref, wq_ref, sk_ref, e_ref, g_ref):
    hb = h_ref[...].astype(BF16)
    tokens = hb.shape[0]
    half = PEER_KEY_DIM // 2
    key_ids = lax.broadcasted_iota(I32, (N_SUBKEYS, tokens), 0).astype(F32)
    sub = lax.broadcasted_iota(I32, (SUBLANES, tokens), 0).astype(F32)
    experts, gates = [], []
    for head in range(PEER_HEADS):
        top = []
        for p in range(2):
            c0 = head * PEER_KEY_DIM + p * half
            qhp = jnp.dot(hb, wq_ref[:, c0:c0 + half], preferred_element_type=F32)
            sc = _dot_t(sk_ref[p], qhp.astype(BF16))
            top.append(_top_rows(sc, key_ids, PEER_TOPK))
        (v1, i1), (v2, i2) = top
        cand, cand_ids = _pair_candidates(v1, v2, sub)
        c_top, c_id = _top_rows(cand, cand_ids, PEER_TOPK)
        rank1 = jnp.floor(c_id * (1.0 / PEER_TOPK))
        rank2 = c_id - rank1 * PEER_TOPK
        experts.append(_take_rows(i1, rank1) * N_SUBKEYS + _take_rows(i2, rank2))
        ex = jnp.exp(c_top - jnp.max(c_top, axis=0, keepdims=True))
        gates.append(ex / jnp.sum(ex, axis=0, keepdims=True))
    e_ref[...] = jnp.concatenate(experts, axis=0).T.astype(I32)
    g_ref[...] = jnp.concatenate(gates, axis=0).T


def _peer_route(h, wq_bf16, subkeys_bf16):
    n = h.shape[0]
    tm = min(ROW_TILE, n)
    row = lambda i: (i, 0)
    return pl.pallas_call(
        _peer_route_kernel,
        grid=(n // tm,),
        in_specs=[pl.BlockSpec((tm, D_MODEL), row), _resident(wq_bf16.shape),
                  _resident(subkeys_bf16.shape)],
        out_specs=[pl.BlockSpec((tm, PEER_SLOTS), row)] * 2,
        out_shape=[jax.ShapeDtypeStruct((n, PEER_SLOTS), I32),
                   jax.ShapeDtypeStruct((n, PEER_SLOTS), F32)],
        compiler_params=_params("parallel"),
        name="peer_route",
    )(h, wq_bf16, subkeys_bf16)


PEER_TOKENS = 64
PEER_RING = 8
PIECE_GROUPS = 2


def _fold(row):
    return jnp.concatenate([row[:, j * LANES:(j + 1) * LANES] for j in range(FOLD)], axis=0)


def _unfold(folded):
    return jnp.concatenate([folded[j:j + 1, :] for j in range(FOLD)], axis=1)


def _sublane_sums(vregs):
    sub = lax.broadcasted_iota(I32, (SUBLANES, LANES), 0)
    roll = lambda a, k: pltpu.roll(a, k, axis=0)
    p = [vregs[i] for i in (3, 2, 1, 0, 7, 6, 5, 4)]
    m = [jnp.where(sub < 4, p[i] + roll(p[i], 4), p[i + 4] + roll(p[i + 4], 4)) for i in range(4)]
    n = [jnp.where((sub & 3) >= 2, m[i] + roll(m[i], 2), roll(m[i + 2] + roll(m[i + 2], 2), 6))
         for i in range(2)]
    return jnp.where((sub & 1) == 1, n[0] + roll(n[0], 1), roll(n[1] + roll(n[1], 1), 7))


def _peer_expert_kernel(e_ref, e_next_ref, h_ref, g_ref, uv_hbm, o_ref, buf, sem, *, steps):
    tokens = h_ref.shape[0]
    ahead = PEER_RING - 1
    assert tokens % PEER_RING == 0 and tokens > ahead and FOLD == 2 * SUBLANES
    step = pl.program_id(0)
    eye = (lax.broadcasted_iota(I32, (PEER_SLOTS, PEER_SLOTS), 0)
           == lax.broadcasted_iota(I32, (PEER_SLOTS, PEER_SLOTS), 1))

    groups = PEER_SLOTS // SUBLANES
    pieces = 2 * groups // PIECE_GROUPS
    per_piece = PEER_SLOTS // pieces

    def start_piece(ids_ref, t, slot, piece):
        for k in range(piece * per_piece, (piece + 1) * per_piece):
            pltpu.make_async_copy(uv_hbm.at[ids_ref[t, k]], buf.at[slot, k], sem.at[slot]).start(priority=k % 2)

    def wait(slot):
        pltpu.make_async_copy(uv_hbm.at[pl.ds(0, PEER_SLOTS)], buf.at[slot], sem.at[slot]).wait()

    def token(ids_ref, t_ahead, slot_ahead, t, slot):
        wait(slot)
        x = _fold(h_ref[pl.ds(t, 1), :])
        sums = []
        for c in range(groups):
            if c % PIECE_GROUPS == 0:
                start_piece(ids_ref, t_ahead, slot_ahead, c // PIECE_GROUPS)
            rows = slice(c * SUBLANES, (c + 1) * SUBLANES)
            prod = buf[slot, rows, 0:FOLD, :].astype(F32) * x[None]
            sums.append(_sublane_sums([prod[i, 0:SUBLANES, :] + prod[i, SUBLANES:FOLD, :]
                                       for i in range(SUBLANES)]))
        hid = jnp.sum(jnp.concatenate(sums, axis=0), axis=1, keepdims=True)
        gate = jnp.sum(jnp.where(eye, g_ref[pl.ds(t, 1), :], 0.0), axis=1, keepdims=True)
        a = gate * (0.5 * hid * (1.0 + lax.erf(hid * (2.0 ** -0.5))))
        a = jnp.broadcast_to(a, (PEER_SLOTS, LANES))
        y = jnp.zeros((FOLD, LANES), F32)
        for c in range(groups):
            if c % PIECE_GROUPS == 0:
                start_piece(ids_ref, t_ahead, slot_ahead, pieces // 2 + c // PIECE_GROUPS)
            rows = slice(c * SUBLANES, (c + 1) * SUBLANES)
            y = y + jnp.sum(a[rows][:, None, :] * buf[slot, rows, FOLD:2 * FOLD, :].astype(F32), axis=0)
        o_ref[pl.ds(t, 1), :] = _unfold(y)

    @pl.when(step == 0)
    def _():
        for t in range(ahead):
            for piece in range(pieces):
                start_piece(e_ref, t, t, piece)

    def body(t, carry):
        token(e_ref, t + ahead, (t + ahead) % PEER_RING, t, t % PEER_RING)
        return carry

    lax.fori_loop(0, tokens - ahead, body, 0)
    for j in range(ahead):
        t = tokens - ahead + j
        token(e_next_ref, j, j % PEER_RING, t, t % PEER_RING)

    @pl.when(step == steps - 1)
    def _():
        for j in range(ahead):
            wait(j % PEER_RING)


def _peer_expert(experts, h, gates, uv_tab):
    n = h.shape[0]
    tb = min(PEER_TOKENS, n)
    steps = n // tb
    row = lambda i: (i, 0)
    return pl.pallas_call(
        functools.partial(_peer_expert_kernel, steps=steps),
        grid=(steps,),
        in_specs=[pl.BlockSpec((tb, PEER_SLOTS), row, memory_space=pltpu.SMEM),
                  pl.BlockSpec((tb, PEER_SLOTS), lambda i: (jnp.minimum(i + 1, steps - 1), 0),
                               memory_space=pltpu.SMEM),
                  pl.BlockSpec((tb, D_MODEL), row),
                  pl.BlockSpec((tb, PEER_SLOTS), row),
                  pl.BlockSpec(memory_space=pl.ANY)],
        out_specs=pl.BlockSpec((tb, D_MODEL), row),
        out_shape=jax.ShapeDtypeStruct((n, D_MODEL), F32),
        scratch_shapes=[pltpu.VMEM((PEER_RING, PEER_SLOTS, 2 * FOLD, LANES), uv_tab.dtype),
                        pltpu.SemaphoreType.DMA((PEER_RING,))],
        compiler_params=_params("arbitrary"),
        name="peer_expert",
    )(experts, experts, h, gates, uv_tab)


def _tail_kernel(h_ref, y_ref, p_ref, g_ref, b_ref, wg_ref, wp_ref, o_ref):
    h2 = _layer_norm(DEEPNORM_ALPHA * h_ref[...] + y_ref[...], g_ref[...], b_ref[...])
    gate = jax.nn.sigmoid(jnp.dot(h2.astype(BF16), wg_ref[...], preferred_element_type=F32))
    emb = jnp.dot(p_ref[...].astype(BF16), wp_ref[...], preferred_element_type=F32)
    o_ref[...] = h2 + gate * emb


def _tail(h, y, p, g, b, wg_bf16, wp_bf16):
    n = h.shape[0]
    tm = min(ROW_TILE, n)
    row = lambda i: (i, 0)
    return pl.pallas_call(
        _tail_kernel,
        grid=(n // tm,),
        in_specs=[pl.BlockSpec((tm, D_MODEL), row), pl.BlockSpec((tm, D_MODEL), row),
                  pl.BlockSpec((tm, p.shape[1]), row), _resident(g.shape), _resident(b.shape),
                  _resident(wg_bf16.shape), _resident(wp_bf16.shape)],
        out_specs=pl.BlockSpec((tm, D_MODEL), row),
        out_shape=jax.ShapeDtypeStruct((n, D_MODEL), F32),
        compiler_params=_params("parallel"),
        name="tail",
    )(h, y, p, g, b, wg_bf16, wp_bf16)


def _rotary_tables(pos):
    half = HEAD_DIM // 2
    inv = ROPE_THETA ** (-jnp.arange(half, dtype=F32) / half)
    ang = pos.astype(F32)[:, None] * inv[None, :]
    cos, sin = jnp.cos(ang), jnp.sin(ang)
    return jnp.concatenate([cos, cos], axis=1), jnp.concatenate([-sin, sin], axis=1)


def _state_halo(state):
    return jnp.pad(state, ((0, 0), (SUBLANES - (CONV_WIDTH - 1), 0), (0, 0)))


def _layer_tail(x, att, conv, p, w):
    h1 = _out_proj(att, conv, x, w["w_out"], w["ln1_g"], w["ln1_b"])
    experts, gates = _peer_route(h1, w["peer_wq"], w["peer_subkeys"])
    y = _peer_expert(experts, h1, gates, w["peer_uv"])
    return _tail(h1, y, p, w["ln2_g"], w["ln2_b"], w["ple_gate"], w["ple_proj"])


def kernel(x_prompt, x_sample, state_attn_k, state_attn_v, state_conv, p_prompt, p_sample,
           w_in, w_conv, w_out, ln1_g, ln1_b, peer_wq, peer_subkeys, peer_u, peer_v,
           ln2_g, ln2_b, ple_gate, ple_proj):
    assert w_in.shape[0] == DEPTH
    bp, tp, _ = x_prompt.shape
    bs, ts, _ = x_sample.shape
    w = {
        "w_in": w_in[0].astype(BF16), "w_out": w_out[0].astype(BF16),
        "ln1_g": ln1_g, "ln1_b": ln1_b, "ln2_g": ln2_g, "ln2_b": ln2_b,
        "peer_wq": peer_wq[0].astype(BF16), "peer_subkeys": peer_subkeys[0].astype(BF16),
        "peer_uv": jnp.concatenate([peer_u.reshape(-1, FOLD, LANES), peer_v.reshape(-1, FOLD, LANES)],
                                   axis=1).astype(BF16),
        "ple_gate": ple_gate[0].astype(BF16), "ple_proj": ple_proj[0].astype(BF16),
    }
    wc = w_conv[0]

    xp = x_prompt.reshape(bp * tp, D_MODEL)
    cos_p, sin_p = _rotary_tables(jnp.arange(tp))
    q, k, v, z, gb = _in_proj(xp, w["w_in"], cos_p, sin_p)
    conv = _conv(z, gb, jnp.zeros((bp, SUBLANES, D_CONV), F32), wc, bp, tp)
    att = _attn_prompt(q, k, v, bp, tp)
    y_prompt = _layer_tail(xp, att, conv, p_prompt[0].reshape(bp * tp, -1), w).reshape(bp, tp, D_MODEL)
    new_k_prompt = k.reshape(1, bp, tp, ATT_HEADS, HEAD_DIM)
    new_v_prompt = v.reshape(1, bp, tp, ATT_HEADS, HEAD_DIM)
    new_conv_prompt = z.reshape(bp, tp, D_CONV)[:, tp - (CONV_WIDTH - 1):][None]

    xs = x_sample.reshape(bs * ts, D_MODEL)
    cos_s, sin_s = _rotary_tables(jnp.tile(PAST_LEN + jnp.arange(ts), bs))
    q, k, v, z, gb = _in_proj(xs, w["w_in"], cos_s, sin_s)
    conv = _conv(z, gb, _state_halo(state_conv[0]), wc, bs, ts)
    att, new_k_sample, new_v_sample = _attn_sample(q, k, v, state_attn_k, state_attn_v, bs, ts)
    y_sample = _layer_tail(xs, att, conv, p_sample[0].reshape(bs * ts, -1), w).reshape(bs, ts, D_MODEL)
    z_ext = jnp.concatenate([state_conv[0], z.reshape(bs, ts, D_CONV)], axis=1)
    new_conv_sample = z_ext[:, -(CONV_WIDTH - 1):][None]

    return (y_prompt, y_sample, new_k_prompt, new_v_prompt, new_conv_prompt,
            new_k_sample, new_v_sample, new_conv_sample)
```

```python
import functools
import math

import jax
import jax.numpy as jnp
from jax import lax
from jax.experimental import pallas as pl
from jax.experimental.pallas import tpu as pltpu

F32 = jnp.float32
BF16 = jnp.bfloat16
I32 = jnp.int32

D_MODEL = 2048
PAST_LEN = 16384
ATT_HEADS = 8
HEAD_DIM = 128
D_ATT = ATT_HEADS * HEAD_DIM
D_CONV = D_MODEL - D_ATT
CONV_WIDTH = 3
DILATIONS = (1, 4, 16)
BAND = 128
ROPE_THETA = 10000.0
PEER_HEADS = 8
N_SUBKEYS = 128
PEER_KEY_DIM = 256
PEER_TOPK = 16
PEER_SLOTS = PEER_HEADS * PEER_TOPK
DEPTH = 1
DEEPNORM_ALPHA = (2.0 * DEPTH) ** 0.25
LN_EPS = 1e-5

LANES = 128
SUBLANES = 8
FOLD = D_MODEL // LANES
ROW_TILE = 256
VMEM_LIMIT = 56 * 1024 * 1024
NEG_INF = float("-inf")


def _params(*sem):
    return pltpu.CompilerParams(dimension_semantics=sem, vmem_limit_bytes=VMEM_LIMIT)


def _resident(shape):
    nd = len(shape)
    return pl.BlockSpec(shape, lambda *_: (0,) * nd, pipeline_mode=pl.Buffered(1))


def _layer_norm(y, g, b):
    mu = jnp.mean(y, axis=-1, keepdims=True)
    c = y - mu
    var = jnp.mean(c * c, axis=-1, keepdims=True)
    return c * lax.rsqrt(var + LN_EPS) * g + b


def _in_proj_kernel(x_ref, w_ref, cos_ref, sin_ref, q_ref, k_ref, v_ref, z_ref, gb_ref):
    xb = x_ref[...].astype(BF16)
    cos = cos_ref[...]
    sin = sin_ref[...]

    def proj(c):
        return jnp.dot(xb, w_ref[:, c * D_ATT:(c + 1) * D_ATT], preferred_element_type=F32)

    def store_rotary(a, o_ref):
        for h in range(ATT_HEADS):
            ah = a[:, h * HEAD_DIM:(h + 1) * HEAD_DIM]
            o_ref[:, h * HEAD_DIM:(h + 1) * HEAD_DIM] = (
                ah * cos + pltpu.roll(ah, HEAD_DIM // 2, axis=1) * sin)

    store_rotary(proj(0), q_ref)
    store_rotary(proj(1), k_ref)
    v_ref[...] = proj(2)
    u = proj(3)
    gb_ref[...] = proj(4)
    z_ref[...] = proj(5) * u


def _in_proj(x, w_bf16, cos_tab, sin_tab):
    n = x.shape[0]
    tm = min(ROW_TILE, n)
    n_tab = cos_tab.shape[0] // tm
    row = lambda i: (i, 0)
    out = jax.ShapeDtypeStruct((n, D_ATT), F32)
    return pl.pallas_call(
        _in_proj_kernel,
        grid=(n // tm,),
        in_specs=[pl.BlockSpec((tm, D_MODEL), row),
                  _resident(w_bf16.shape),
                  pl.BlockSpec((tm, HEAD_DIM), lambda i: (i % n_tab, 0)),
                  pl.BlockSpec((tm, HEAD_DIM), lambda i: (i % n_tab, 0))],
        out_specs=[pl.BlockSpec((tm, D_ATT), row)] * 5,
        out_shape=[out] * 5,
        compiler_params=_params("parallel"),
        name="in_proj",
    )(x, w_bf16, cos_tab, sin_tab)


def _conv_kernel(z_ref, zprev_ref, st_ref, gb_ref, wc_ref, o_ref):
    halo = jnp.where(pl.program_id(1) == 0, st_ref[...], zprev_ref[...])
    z = z_ref[...]
    row = lax.broadcasted_iota(I32, z.shape, 0)
    h1 = halo[SUBLANES - 1:SUBLANES, :]
    h2 = halo[SUBLANES - 2:SUBLANES - 1, :]
    zm1 = jnp.where(row == 0, h1, pltpu.roll(z, 1, axis=0))
    zm2 = jnp.where(row == 0, h2, jnp.where(row == 1, h1, pltpu.roll(z, 2, axis=0)))
    wc = wc_ref[...]
    o_ref[...] = gb_ref[...] * (wc[0:1, :] * zm2 + wc[1:2, :] * zm1 + wc[2:3, :] * z)


def _conv(z, gb, state_halo, w_conv, batch, seq):
    tt = min(ROW_TILE, seq)
    per_seq = seq // tt
    halo_per_tile = tt // SUBLANES
    z3 = z.reshape(batch, seq, D_CONV)
    gb3 = gb.reshape(batch, seq, D_CONV)
    tile = pl.BlockSpec((None, tt, D_CONV), lambda b, i: (b, i, 0))
    out = pl.pallas_call(
        _conv_kernel,
        grid=(batch, per_seq),
        in_specs=[tile,
                  pl.BlockSpec((None, SUBLANES, D_CONV),
                               lambda b, i: (b, jnp.maximum(i * halo_per_tile - 1, 0), 0)),
                  pl.BlockSpec((None, SUBLANES, D_CONV), lambda b, i: (b, 0, 0)),
                  tile,
                  pl.BlockSpec((CONV_WIDTH, D_CONV), lambda b, i: (0, 0))],
        out_specs=tile,
        out_shape=jax.ShapeDtypeStruct((batch, seq, D_CONV), F32),
        compiler_params=_params("parallel", "parallel"),
        name="conv",
    )(z3, z3, state_halo, gb3, w_conv)
    return out.reshape(batch * seq, D_CONV)


def _dot_t(a, b):
    return lax.dot_general(a, b, (((1,), (1,)), ((), ())), preferred_element_type=F32)


ATTN_UNROLL = 8


def _attn_prompt_kernel(q_ref, k_ref, v_ref, o_ref, acc_sc, m_sc, l_sc):
    seq = q_ref.shape[0]
    scale = HEAD_DIM ** -0.5
    qi = lax.broadcasted_iota(I32, (BAND, 2 * BAND), 0)
    kj = lax.broadcasted_iota(I32, (BAND, 2 * BAND), 1)
    band_ok = (kj >= qi) & (kj <= qi + BAND)
    is_prev = kj < BAND
    own_ok = (lax.broadcasted_iota(I32, (BAND, BAND), 1) <= lax.broadcasted_iota(I32, (BAND, BAND), 0))

    for bi, d in enumerate(DILATIONS):
        nb = seq // (d * BAND)

        def block(blk, carry, bi=bi, d=d, nb=nb):
            r = blk // nb
            n = blk % nb
            start = r + n * (d * BAND)
            rows = pl.ds(start, BAND, stride=d) if d > 1 else pl.ds(pl.multiple_of(start, BAND), BAND)
            q = q_ref[rows, :].astype(BF16)
            if nb > 1:
                prev_start = jnp.maximum(start - d * BAND, 0)
                prows = (pl.ds(prev_start, BAND, stride=d) if d > 1
                         else pl.ds(pl.multiple_of(prev_start, BAND), BAND))
                keys = jnp.concatenate([k_ref[prows, :], k_ref[rows, :]], axis=0).astype(BF16)
                vals = jnp.concatenate([v_ref[prows, :], v_ref[rows, :]], axis=0)
                s = (jnp.where(band_ok, _dot_t(q, keys) * scale, NEG_INF)
                     + jnp.where(is_prev, jnp.where(n > 0, 0.0, NEG_INF), 0.0))
            else:
                keys = k_ref[rows, :].astype(BF16)
                vals = v_ref[rows, :]
                s = jnp.where(own_ok, _dot_t(q, keys) * scale, NEG_INF)
            m = jnp.max(s, axis=1, keepdims=True)
            p = jnp.exp(s - m).astype(BF16)
            vals_ones = jnp.concatenate([vals, jnp.ones(vals.shape, F32)], axis=1).astype(BF16)
            ext = jnp.dot(p, vals_ones, preferred_element_type=F32)
            acc_sc[bi, rows, :] = ext[:, :HEAD_DIM]
            l_sc[bi, rows, :] = ext[:, HEAD_DIM:]
            m_sc[bi, rows, :] = jnp.broadcast_to(m, (BAND, HEAD_DIM))
            return carry

        lax.fori_loop(0, d * nb, block, 0, unroll=ATTN_UNROLL)

    big_m = jnp.maximum(jnp.maximum(m_sc[0], m_sc[1]), m_sc[2])
    num = jnp.zeros((seq, HEAD_DIM), F32)
    den = jnp.zeros((seq, HEAD_DIM), F32)
    for bi in range(len(DILATIONS)):
        w = jnp.exp(m_sc[bi] - big_m)
        num = num + w * acc_sc[bi]
        den = den + w * l_sc[bi]
    o_ref[...] = num / den


def _attn_prompt(q, k, v, batch, seq):
    q3, k3, v3 = (a.reshape(batch, seq, D_ATT) for a in (q, k, v))
    head = pl.BlockSpec((None, seq, HEAD_DIM), lambda b, h: (b, 0, h))
    nbr = len(DILATIONS)
    out = pl.pallas_call(
        _attn_prompt_kernel,
        grid=(batch, ATT_HEADS),
        in_specs=[head, head, head],
        out_specs=head,
        out_shape=jax.ShapeDtypeStruct((batch, seq, D_ATT), F32),
        scratch_shapes=[pltpu.VMEM((nbr, seq, HEAD_DIM), F32)] * 3,
        compiler_params=_params("parallel", "parallel"),
        name="attn_prompt",
    )(q3, k3, v3)
    return out.reshape(batch * seq, D_ATT)


def _branch_count(rel):
    cnt = jnp.zeros(rel.shape, F32)
    for d in DILATIONS:
        hit = (rel >= 0) & (rel <= d * BAND) & ((rel & (d - 1)) == 0)
        cnt = cnt + jnp.where(hit, 1.0, 0.0)
    return cnt


SAMPLE_CHUNK = 512
M_FLOOR = -1e30


def _attn_sample_kernel(q_ref, kn_ref, vn_ref, kn3_ref, vn3_ref, ks_ref, ks_next_ref, vs_ref, vs_next_ref,
                        o_ref, ko_ref, vo_ref, m_sc, l_sc, acc_sc, *, n_buf, chunks):
    c = pl.program_id(1)
    rows = ks_ref.shape[0]
    t_new = q_ref.shape[0]
    scale = HEAD_DIM ** -0.5
    log_heads = int(math.log2(ATT_HEADS))
    log_new = int(math.log2(t_new))
    assert ATT_HEADS == 1 << log_heads and t_new == 1 << log_new

    def by_head(ref):
        return jnp.concatenate([ref[:, h * HEAD_DIM:(h + 1) * HEAD_DIM] for h in range(ATT_HEADS)], axis=0)

    @pl.when(c == 0)
    def _():
        m_sc[...] = jnp.full(m_sc.shape, M_FLOOR, F32)
        l_sc[...] = jnp.zeros(l_sc.shape, F32)
        acc_sc[...] = jnp.zeros(acc_sc.shape, F32)

    q = by_head(q_ref).astype(BF16)

    def accumulate(k2, v2, key_head, rel):
        q_head = lax.shift_right_logical(lax.broadcasted_iota(I32, rel.shape, 0), log_new)
        cnt = jnp.where(key_head == q_head, _branch_count(rel), 0.0)
        s = jnp.where(cnt > 0, _dot_t(q, k2.astype(BF16)) * scale, NEG_INF)
        m_old = m_sc[...]
        m_new = jnp.maximum(m_old, jnp.max(s, axis=1, keepdims=True))
        alpha = jnp.exp(m_old - m_new)
        p = cnt * jnp.exp(s - m_new)
        l_sc[...] = alpha * l_sc[...] + jnp.sum(p, axis=1, keepdims=True)
        acc_sc[...] = alpha * acc_sc[...] + jnp.dot(p.astype(BF16), v2.astype(BF16),
                                                    preferred_element_type=F32)
        m_sc[...] = m_new

    ks = ks_ref[...]
    vs = vs_ref[...]
    n = rows * ATT_HEADS
    col = lax.broadcasted_iota(I32, (ATT_HEADS * t_new, n), 1)
    qrow = lax.broadcasted_iota(I32, (ATT_HEADS * t_new, n), 0)
    kpos = c * rows + lax.shift_right_logical(col, log_heads)
    qpos = n_buf + (qrow & (t_new - 1))
    accumulate(ks.reshape(n, HEAD_DIM), vs.reshape(n, HEAD_DIM), col & (ATT_HEADS - 1), qpos - kpos)

    last = c == chunks - 1
    ko_ref[0:rows - t_new] = ks[t_new:]
    vo_ref[0:rows - t_new] = vs[t_new:]
    ko_ref[rows - t_new:rows] = jnp.where(last, kn3_ref[...], ks_next_ref[...])
    vo_ref[rows - t_new:rows] = jnp.where(last, vn3_ref[...], vs_next_ref[...])

    @pl.when(last)
    def _():
        m2 = ATT_HEADS * t_new
        ncol = lax.broadcasted_iota(I32, (m2, m2), 1)
        nrow = lax.broadcasted_iota(I32, (m2, m2), 0)
        accumulate(by_head(kn_ref), by_head(vn_ref), lax.shift_right_logical(ncol, log_new),
                   (nrow & (t_new - 1)) - (ncol & (t_new - 1)))
        out = acc_sc[...] / l_sc[...]
        for h in range(ATT_HEADS):
            o_ref[:, h * HEAD_DIM:(h + 1) * HEAD_DIM] = out[h * t_new:(h + 1) * t_new, :]


def _attn_sample(q, k_new, v_new, k_state, v_state, batch, t_new):
    n_buf = k_state.shape[2]
    rows = min(SAMPLE_CHUNK, n_buf)
    chunks = n_buf // rows
    assert t_new == SUBLANES and rows % t_new == 0
    kn3 = k_new.reshape(batch * t_new, ATT_HEADS, HEAD_DIM)
    vn3 = v_new.reshape(batch * t_new, ATT_HEADS, HEAD_DIM)
    new = pl.BlockSpec((t_new, D_ATT), lambda b, c: (b, 0))
    new3 = pl.BlockSpec((t_new, ATT_HEADS, HEAD_DIM), lambda b, c: (b, 0, 0))
    buf = pl.BlockSpec((None, None, rows, ATT_HEADS, HEAD_DIM), lambda b, c: (0, b, c, 0, 0))
    per = rows // t_new
    nxt = pl.BlockSpec((None, None, t_new, ATT_HEADS, HEAD_DIM),
                       lambda b, c: (0, b, jnp.minimum((c + 1) * per, n_buf // t_new - 1), 0, 0))
    buf_shape = jax.ShapeDtypeStruct(k_state.shape, F32)
    qrows = ATT_HEADS * t_new
    return pl.pallas_call(
        functools.partial(_attn_sample_kernel, n_buf=n_buf, chunks=chunks),
        grid=(batch, chunks),
        in_specs=[new, new, new, new3, new3, buf, nxt, buf, nxt],
        out_specs=[new, buf, buf],
        out_shape=[jax.ShapeDtypeStruct((batch * t_new, D_ATT), F32), buf_shape, buf_shape],
        scratch_shapes=[pltpu.VMEM((qrows, 1), F32), pltpu.VMEM((qrows, 1), F32),
                        pltpu.VMEM((qrows, HEAD_DIM), F32)],
        compiler_params=_params("parallel", "arbitrary"),
        name="attn_sample",
    )(q, k_new, v_new, kn3, vn3, k_state, k_state, v_state, v_state)


def _out_proj_kernel(att_ref, conv_ref, x_ref, w_ref, g_ref, b_ref, o_ref):
    mixed = (jnp.dot(att_ref[...].astype(BF16), w_ref[0:D_ATT, :], preferred_element_type=F32)
             + jnp.dot(conv_ref[...].astype(BF16), w_ref[D_ATT:D_MODEL, :], preferred_element_type=F32))
    o_ref[...] = _layer_norm(DEEPNORM_ALPHA * x_ref[...] + mixed, g_ref[...], b_ref[...])


def _out_proj(att, conv, x, w_bf16, g, b):
    n = x.shape[0]
    tm = min(ROW_TILE, n)
    row = lambda i: (i, 0)
    return pl.pallas_call(
        _out_proj_kernel,
        grid=(n // tm,),
        in_specs=[pl.BlockSpec((tm, D_ATT), row), pl.BlockSpec((tm, D_CONV), row),
                  pl.BlockSpec((tm, D_MODEL), row), _resident(w_bf16.shape),
                  _resident(g.shape), _resident(b.shape)],
        out_specs=pl.BlockSpec((tm, D_MODEL), row),
        out_shape=jax.ShapeDtypeStruct((n, D_MODEL), F32),
        compiler_params=_params("parallel"),
        name="out_proj",
    )(att, conv, x, w_bf16, g, b)


ID_NONE = float(2 ** 24)


def _top_rows(s, ids, k):
    vals, sel = [], []
    for _ in range(k):
        m = jnp.max(s, axis=0, keepdims=True)
        i = jnp.min(jnp.where(s == m, ids, ID_NONE), axis=0, keepdims=True)
        vals.append(m)
        sel.append(i)
        s = jnp.where(ids == i, NEG_INF, s)
    return jnp.concatenate(vals, axis=0), jnp.concatenate(sel, axis=0)


def _pair_candidates(v1, v2, sub):
    assert PEER_TOPK == 2 * SUBLANES
    vals = [v1[0:1, :] + v2]
    ids = [sub, sub + SUBLANES]
    for i in range(1, SUBLANES):
        vals.append(v1[i:i + 1, :] + v2[0:SUBLANES, :])
        ids.append(sub + i * PEER_TOPK)
    vals.append(v1[SUBLANES:, :] + v2[0:1, :])
    ids.append((sub + SUBLANES) * PEER_TOPK)
    return jnp.concatenate(vals, axis=0), jnp.concatenate(ids, axis=0)


def _take_rows(table, idx):
    out = jnp.zeros(idx.shape, table.dtype)
    for i in range(table.shape[0]):
        out = jnp.where(idx == i, table[i:i + 1, :], out)
    return out


def _peer_route_kernel(h_ref, wq_ref, sk_ref, e_ref, g_ref):
    hb = h_ref[...].astype(BF16)
    tokens = hb.shape[0]
    half = PEER_KEY_DIM // 2
    key_ids = lax.broadcasted_iota(I32, (N_SUBKEYS, tokens), 0).astype(F32)
    sub = lax.broadcasted_iota(I32, (SUBLANES, tokens), 0).astype(F32)
    experts, gates = [], []
    for head in range(PEER_HEADS):
        top = []
        for p in range(2):
            c0 = head * PEER_KEY_DIM + p * half
            qhp = jnp.dot(hb, wq_ref[:, c0:c0 + half], preferred_element_type=F32)
            sc = _dot_t(sk_ref[p], qhp.astype(BF16))
            top.append(_top_rows(sc, key_ids, PEER_TOPK))
        (v1, i1), (v2, i2) = top
        cand, cand_ids = _pair_candidates(v1, v2, sub)
        c_top, c_id = _top_rows(cand, cand_ids, PEER_TOPK)
        rank1 = jnp.floor(c_id * (1.0 / PEER_TOPK))
        rank2 = c_id - rank1 * PEER_TOPK
        experts.append(_take_rows(i1, rank1) * N_SUBKEYS + _take_rows(i2, rank2))
        ex = jnp.exp(c_top - jnp.max(c_top, axis=0, keepdims=True))
        gates.append(ex / jnp.sum(ex, axis=0, keepdims=True))
    e_ref[...] = jnp.concatenate(experts, axis=0).T.astype(I32)
    g_ref[...] = jnp.concatenate(gates, axis=0).T


def _peer_route(h, wq_bf16, subkeys_bf16):
    n = h.shape[0]
    tm = min(ROW_TILE, n)
    row = lambda i: (i, 0)
    return pl.pallas_call(
        _peer_route_kernel,
        grid=(n // tm,),
        in_specs=[pl.BlockSpec((tm, D_MODEL), row), _resident(wq_bf16.shape),
                  _resident(subkeys_bf16.shape)],
        out_specs=[pl.BlockSpec((tm, PEER_SLOTS), row)] * 2,
        out_shape=[jax.ShapeDtypeStruct((n, PEER_SLOTS), I32),
                   jax.ShapeDtypeStruct((n, PEER_SLOTS), F32)],
        compiler_params=_params("parallel"),
        name="peer_route",
    )(h, wq_bf16, subkeys_bf16)


PEER_TOKENS = 64
PEER_RING = 8
PIECE_GROUPS = 2


def _fold(row):
    return jnp.concatenate([row[:, j * LANES:(j + 1) * LANES] for j in range(FOLD)], axis=0)


def _unfold(folded):
    return jnp.concatenate([folded[j:j + 1, :] for j in range(FOLD)], axis=1)


def _sublane_sums(vregs):
    sub = lax.broadcasted_iota(I32, (SUBLANES, LANES), 0)
    roll = lambda a, k: pltpu.roll(a, k, axis=0)
    p = [vregs[i] for i in (3, 2, 1, 0, 7, 6, 5, 4)]
    m = [jnp.where(sub < 4, p[i] + roll(p[i], 4), p[i + 4] + roll(p[i + 4], 4)) for i in range(4)]
    n = [jnp.where((sub & 3) >= 2, m[i] + roll(m[i], 2), roll(m[i + 2] + roll(m[i + 2], 2), 6))
         for i in range(2)]
    return jnp.where((sub & 1) == 1, n[0] + roll(n[0], 1), roll(n[1] + roll(n[1], 1), 7))


def _peer_expert_kernel(e_ref, e_next_ref, h_ref, g_ref, uv_hbm, o_ref, buf, sem, *, steps):
    tokens = h_ref.shape[0]
    ahead = PEER_RING - 1
    assert tokens % PEER_RING == 0 and tokens > ahead and FOLD == 2 * SUBLANES
    step = pl.program_id(0)
    eye = (lax.broadcasted_iota(I32, (PEER_SLOTS, PEER_SLOTS), 0)
           == lax.broadcasted_iota(I32, (PEER_SLOTS, PEER_SLOTS), 1))

    groups = PEER_SLOTS // SUBLANES
    pieces = 2 * groups // PIECE_GROUPS
    per_piece = PEER_SLOTS // pieces

    def start_piece(ids_ref, t, slot, piece):
        for k in range(piece * per_piece, (piece + 1) * per_piece):
            pltpu.make_async_copy(uv_hbm.at[ids_ref[t, k]], buf.at[slot, k], sem.at[slot]).start(priority=k % 2)

    def wait(slot):
        pltpu.make_async_copy(uv_hbm.at[pl.ds(0, PEER_SLOTS)], buf.at[slot], sem.at[slot]).wait()

    def token(ids_ref, t_ahead, slot_ahead, t, slot):
        wait(slot)
        x = _fold(h_ref[pl.ds(t, 1), :])
        sums = []
        for c in range(groups):
            if c % PIECE_GROUPS == 0:
                start_piece(ids_ref, t_ahead, slot_ahead, c // PIECE_GROUPS)
            rows = slice(c * SUBLANES, (c + 1) * SUBLANES)
            prod = buf[slot, rows, 0:FOLD, :].astype(F32) * x[None]
            sums.append(_sublane_sums([prod[i, 0:SUBLANES, :] + prod[i, SUBLANES:FOLD, :]
                                       for i in range(SUBLANES)]))
        hid = jnp.sum(jnp.concatenate(sums, axis=0), axis=1, keepdims=True)
        gate = jnp.sum(jnp.where(eye, g_ref[pl.ds(t, 1), :], 0.0), axis=1, keepdims=True)
        a = gate * (0.5 * hid * (1.0 + lax.erf(hid * (2.0 ** -0.5))))
        a = jnp.broadcast_to(a, (PEER_SLOTS, LANES))
        y = jnp.zeros((FOLD, LANES), F32)
        for c in range(groups):
            if c % PIECE_GROUPS == 0:
                start_piece(ids_ref, t_ahead, slot_ahead, pieces // 2 + c // PIECE_GROUPS)
            rows = slice(c * SUBLANES, (c + 1) * SUBLANES)
            y = y + jnp.sum(a[rows][:, None, :] * buf[slot, rows, FOLD:2 * FOLD, :].astype(F32), axis=0)
        o_ref[pl.ds(t, 1), :] = _unfold(y)

    @pl.when(step == 0)
    def _():
        for t in range(ahead):
            for piece in range(pieces):
                start_piece(e_ref, t, t, piece)

    def body(t, carry):
        token(e_ref, t + ahead, (t + ahead) % PEER_RING, t, t % PEER_RING)
        return carry

    lax.fori_loop(0, tokens - ahead, body, 0)
    for j in range(ahead):
        t = tokens - ahead + j
        token(e_next_ref, j, j % PEER_RING, t, t % PEER_RING)

    @pl.when(step == steps - 1)
    def _():
        for j in range(ahead):
            wait(j % PEER_RING)


def _peer_expert(experts, h, gates, uv_tab):
    n = h.shape[0]
    tb = min(PEER_TOKENS, n)
    steps = n // tb
    row = lambda i: (i, 0)
    return pl.pallas_call(
        functools.partial(_peer_expert_kernel, steps=steps),
        grid=(steps,),
        in_specs=[pl.BlockSpec((tb, PEER_SLOTS), row, memory_space=pltpu.SMEM),
                  pl.BlockSpec((tb, PEER_SLOTS), lambda i: (jnp.minimum(i + 1, steps - 1), 0),
                               memory_space=pltpu.SMEM),
                  pl.BlockSpec((tb, D_MODEL), row),
                  pl.BlockSpec((tb, PEER_SLOTS), row),
                  pl.BlockSpec(memory_space=pl.ANY)],
        out_specs=pl.BlockSpec((tb, D_MODEL), row),
        out_shape=jax.ShapeDtypeStruct((n, D_MODEL), F32),
        scratch_shapes=[pltpu.VMEM((PEER_RING, PEER_SLOTS, 2 * FOLD, LANES), uv_tab.dtype),
                        pltpu.SemaphoreType.DMA((PEER_RING,))],
        compiler_params=_params("arbitrary"),
        name="peer_expert",
    )(experts, experts, h, gates, uv_tab)


def _tail_kernel(h_ref, y_ref, p_ref, g_ref, b_ref, wg_ref, wp_ref, o_ref):
    h2 = _layer_norm(DEEPNORM_ALPHA * h_ref[...] + y_ref[...], g_ref[...], b_ref[...])
    gate = jax.nn.sigmoid(jnp.dot(h2.astype(BF16), wg_ref[...], preferred_element_type=F32))
    emb = jnp.dot(p_ref[...].astype(BF16), wp_ref[...], preferred_element_type=F32)
    o_ref[...] = h2 + gate * emb


def _tail(h, y, p, g, b, wg_bf16, wp_bf16):
    n = h.shape[0]
    tm = min(ROW_TILE, n)
    row = lambda i: (i, 0)
    return pl.pallas_call(
        _tail_kernel,
        grid=(n // tm,),
        in_specs=[pl.BlockSpec((tm, D_MODEL), row), pl.BlockSpec((tm, D_MODEL), row),
                  pl.BlockSpec((tm, p.shape[1]), row), _resident(g.shape), _resident(b.shape),
                  _resident(wg_bf16.shape), _resident(wp_bf16.shape)],
        out_specs=pl.BlockSpec((tm, D_MODEL), row),
        out_shape=jax.ShapeDtypeStruct((n, D_MODEL), F32),
        compiler_params=_params("parallel"),
        name="tail",
    )(h, y, p, g, b, wg_bf16, wp_bf16)


def _rotary_tables(pos):
    half = HEAD_DIM // 2
    inv = ROPE_THETA ** (-jnp.arange(half, dtype=F32) / half)
    ang = pos.astype(F32)[:, None] * inv[None, :]
    cos, sin = jnp.cos(ang), jnp.sin(ang)
    return jnp.concatenate([cos, cos], axis=1), jnp.concatenate([-sin, sin], axis=1)


def _state_halo(state):
    return jnp.pad(state, ((0, 0), (SUBLANES - (CONV_WIDTH - 1), 0), (0, 0)))


def _layer_tail(x, att, conv, p, w):
    h1 = _out_proj(att, conv, x, w["w_out"], w["ln1_g"], w["ln1_b"])
    experts, gates = _peer_route(h1, w["peer_wq"], w["peer_subkeys"])
    y = _peer_expert(experts, h1, gates, w["peer_uv"])
    return _tail(h1, y, p, w["ln2_g"], w["ln2_b"], w["ple_gate"], w["ple_proj"])


def kernel(x_prompt, x_sample, state_attn_k, state_attn_v, state_conv, p_prompt, p_sample,
           w_in, w_conv, w_out, ln1_g, ln1_b, peer_wq, peer_subkeys, peer_u, peer_v,
           ln2_g, ln2_b, ple_gate, ple_proj):
    assert w_in.shape[0] == DEPTH
    bp, tp, _ = x_prompt.shape
    bs, ts, _ = x_sample.shape
    w = {
        "w_in": w_in[0].astype(BF16), "w_out": w_out[0].astype(BF16),
        "ln1_g": ln1_g, "ln1_b": ln1_b, "ln2_g": ln2_g, "ln2_b": ln2_b,
        "peer_wq": peer_wq[0].astype(BF16), "peer_subkeys": peer_subkeys[0].astype(BF16),
        "peer_uv": jnp.concatenate([peer_u.reshape(-1, FOLD, LANES), peer_v.reshape(-1, FOLD, LANES)],
                                   axis=1).astype(BF16),
        "ple_gate": ple_gate[0].astype(BF16), "ple_proj": ple_proj[0].astype(BF16),
    }
    wc = w_conv[0]

    xp = x_prompt.reshape(bp * tp, D_MODEL)
    cos_p, sin_p = _rotary_tables(jnp.arange(tp))
    q, k, v, z, gb = _in_proj(xp, w["w_in"], cos_p, sin_p)
    conv = _conv(z, gb, jnp.zeros((bp, SUBLANES, D_CONV), F32), wc, bp, tp)
    att = _attn_prompt(q, k, v, bp, tp)
    y_prompt = _layer_tail(xp, att, conv, p_prompt[0].reshape(bp * tp, -1), w).reshape(bp, tp, D_MODEL)
    new_k_prompt = k.reshape(1, bp, tp, ATT_HEADS, HEAD_DIM)
    new_v_prompt = v.reshape(1, bp, tp, ATT_HEADS, HEAD_DIM)
    new_conv_prompt = z.reshape(bp, tp, D_CONV)[:, tp - (CONV_WIDTH - 1):][None]

    xs = x_sample.reshape(bs * ts, D_MODEL)
    cos_s, sin_s = _rotary_tables(jnp.tile(PAST_LEN + jnp.arange(ts), bs))
    q, k, v, z, gb = _in_proj(xs, w["w_in"], cos_s, sin_s)
    conv = _conv(z, gb, _state_halo(state_conv[0]), wc, bs, ts)
    att, new_k_sample, new_v_sample = _attn_sample(q, k, v, state_attn_k, state_attn_v, bs, ts)
    y_sample = _layer_tail(xs, att, conv, p_sample[0].reshape(bs * ts, -1), w).reshape(bs, ts, D_MODEL)
    z_ext = jnp.concatenate([state_conv[0], z.reshape(bs, ts, D_CONV)], axis=1)
    new_conv_sample = z_ext[:, -(CONV_WIDTH - 1):][None]

    return (y_prompt, y_sample, new_k_prompt, new_v_prompt, new_conv_prompt,
            new_k_sample, new_v_sample, new_conv_sample)
```

```python
import functools
import math

import jax
import jax.numpy as jnp
from jax import lax
from jax.experimental import pallas as pl
from jax.experimental.pallas import tpu as pltpu

F32 = jnp.float32
BF16 = jnp.bfloat16
I32 = jnp.int32

D_MODEL = 2048
PAST_LEN = 16384
ATT_HEADS = 8
HEAD_DIM = 128
D_ATT = ATT_HEADS * HEAD_DIM
D_CONV = D_MODEL - D_ATT
CONV_WIDTH = 3
DILATIONS = (1, 4, 16)
BAND = 128
ROPE_THETA = 10000.0
PEER_HEADS = 8
N_SUBKEYS = 128
PEER_KEY_DIM = 256
PEER_TOPK = 16
PEER_SLOTS = PEER_HEADS * PEER_TOPK
DEPTH = 1
DEEPNORM_ALPHA = (2.0 * DEPTH) ** 0.25
LN_EPS = 1e-5

LANES = 128
SUBLANES = 8
FOLD = D_MODEL // LANES
ROW_TILE = 256
VMEM_LIMIT = 56 * 1024 * 1024
NEG_INF = float("-inf")


def _params(*sem):
    return pltpu.CompilerParams(dimension_semantics=sem, vmem_limit_bytes=VMEM_LIMIT)


def _resident(shape):
    nd = len(shape)
    return pl.BlockSpec(shape, lambda *_: (0,) * nd, pipeline_mode=pl.Buffered(1))


def _layer_norm(y, g, b):
    mu = jnp.mean(y, axis=-1, keepdims=True)
    c = y - mu
    var = jnp.mean(c * c, axis=-1, keepdims=True)
    return c * lax.rsqrt(var + LN_EPS) * g + b


def _in_proj_kernel(x_ref, w_ref, cos_ref, sin_ref, q_ref, k_ref, v_ref, z_ref, gb_ref):
    xb = x_ref[...].astype(BF16)
    cos = cos_ref[...]
    sin = sin_ref[...]

    def proj(c):
        return jnp.dot(xb, w_ref[:, c * D_ATT:(c + 1) * D_ATT], preferred_element_type=F32)

    def store_rotary(a, o_ref):
        for h in range(ATT_HEADS):
            ah = a[:, h * HEAD_DIM:(h + 1) * HEAD_DIM]
            o_ref[:, h * HEAD_DIM:(h + 1) * HEAD_DIM] = (
                ah * cos + pltpu.roll(ah, HEAD_DIM // 2, axis=1) * sin)

    store_rotary(proj(0), q_ref)
    store_rotary(proj(1), k_ref)
    v_ref[...] = proj(2)
    u = proj(3)
    gb_ref[...] = proj(4)
    z_ref[...] = proj(5) * u


def _in_proj(x, w_bf16, cos_tab, sin_tab):
    n = x.shape[0]
    tm = min(ROW_TILE, n)
    n_tab = cos_tab.shape[0] // tm
    row = lambda i: (i, 0)
    out = jax.ShapeDtypeStruct((n, D_ATT), F32)
    return pl.pallas_call(
        _in_proj_kernel,
        grid=(n // tm,),
        in_specs=[pl.BlockSpec((tm, D_MODEL), row),
                  _resident(w_bf16.shape),
                  pl.BlockSpec((tm, HEAD_DIM), lambda i: (i % n_tab, 0)),
                  pl.BlockSpec((tm, HEAD_DIM), lambda i: (i % n_tab, 0))],
        out_specs=[pl.BlockSpec((tm, D_ATT), row)] * 5,
        out_shape=[out] * 5,
        compiler_params=_params("parallel"),
        name="in_proj",
    )(x, w_bf16, cos_tab, sin_tab)


def _conv_kernel(z_ref, zprev_ref, st_ref, gb_ref, wc_ref, o_ref):
    halo = jnp.where(pl.program_id(1) == 0, st_ref[...], zprev_ref[...])
    z = z_ref[...]
    row = lax.broadcasted_iota(I32, z.shape, 0)
    h1 = halo[SUBLANES - 1:SUBLANES, :]
    h2 = halo[SUBLANES - 2:SUBLANES - 1, :]
    zm1 = jnp.where(row == 0, h1, pltpu.roll(z, 1, axis=0))
    zm2 = jnp.where(row == 0, h2, jnp.where(row == 1, h1, pltpu.roll(z, 2, axis=0)))
    wc = wc_ref[...]
    o_ref[...] = gb_ref[...] * (wc[0:1, :] * zm2 + wc[1:2, :] * zm1 + wc[2:3, :] * z)


def _conv(z, gb, state_halo, w_conv, batch, seq):
    tt = min(ROW_TILE, seq)
    per_seq = seq // tt
    halo_per_tile = tt // SUBLANES
    z3 = z.reshape(batch, seq, D_CONV)
    gb3 = gb.reshape(batch, seq, D_CONV)
    tile = pl.BlockSpec((None, tt, D_CONV), lambda b, i: (b, i, 0))
    out = pl.pallas_call(
        _conv_kernel,
        grid=(batch, per_seq),
        in_specs=[tile,
                  pl.BlockSpec((None, SUBLANES, D_CONV),
                               lambda b, i: (b, jnp.maximum(i * halo_per_tile - 1, 0), 0)),
                  pl.BlockSpec((None, SUBLANES, D_CONV), lambda b, i: (b, 0, 0)),
                  tile,
                  pl.BlockSpec((CONV_WIDTH, D_CONV), lambda b, i: (0, 0))],
        out_specs=tile,
        out_shape=jax.ShapeDtypeStruct((batch, seq, D_CONV), F32),
        compiler_params=_params("parallel", "parallel"),
        name="conv",
    )(z3, z3, state_halo, gb3, w_conv)
    return out.reshape(batch * seq, D_CONV)


def _dot_t(a, b):
    return lax.dot_general(a, b, (((1,), (1,)), ((), ())), preferred_element_type=F32)


ATTN_UNROLL = 8


def _attn_prompt_kernel(q_ref, k_ref, v_ref, o_ref, acc_sc, m_sc, l_sc):
    seq = q_ref.shape[0]
    scale = HEAD_DIM ** -0.5
    qi = lax.broadcasted_iota(I32, (BAND, 2 * BAND), 0)
    kj = lax.broadcasted_iota(I32, (BAND, 2 * BAND), 1)
    band_ok = (kj >= qi) & (kj <= qi + BAND)
    is_prev = kj < BAND
    own_ok = (lax.broadcasted_iota(I32, (BAND, BAND), 1) <= lax.broadcasted_iota(I32, (BAND, BAND), 0))

    for bi, d in enumerate(DILATIONS):
        nb = seq // (d * BAND)

        def block(blk, carry, bi=bi, d=d, nb=nb):
            r = blk // nb
            n = blk % nb
            start = r + n * (d * BAND)
            rows = pl.ds(start, BAND, stride=d) if d > 1 else pl.ds(pl.multiple_of(start, BAND), BAND)
            q = q_ref[rows, :].astype(BF16)
            if nb > 1:
                prev_start = jnp.maximum(start - d * BAND, 0)
                prows = (pl.ds(prev_start, BAND, stride=d) if d > 1
                         else pl.ds(pl.multiple_of(prev_start, BAND), BAND))
                keys = jnp.concatenate([k_ref[prows, :], k_ref[rows, :]], axis=0).astype(BF16)
                vals = jnp.concatenate([v_ref[prows, :], v_ref[rows, :]], axis=0)
                s = (jnp.where(band_ok, _dot_t(q, keys) * scale, NEG_INF)
                     + jnp.where(is_prev, jnp.where(n > 0, 0.0, NEG_INF), 0.0))
            else:
                keys = k_ref[rows, :].astype(BF16)
                vals = v_ref[rows, :]
                s = jnp.where(own_ok, _dot_t(q, keys) * scale, NEG_INF)
            m = jnp.max(s, axis=1, keepdims=True)
            p = jnp.exp(s - m).astype(BF16)
            vals_ones = jnp.concatenate([vals, jnp.ones(vals.shape, F32)], axis=1).astype(BF16)
            ext = jnp.dot(p, vals_ones, preferred_element_type=F32)
            acc_sc[bi, rows, :] = ext[:, :HEAD_DIM]
            l_sc[bi, rows, :] = ext[:, HEAD_DIM:]
            m_sc[bi, rows, :] = jnp.broadcast_to(m, (BAND, HEAD_DIM))
            return carry

        lax.fori_loop(0, d * nb, block, 0, unroll=ATTN_UNROLL)

    big_m = jnp.maximum(jnp.maximum(m_sc[0], m_sc[1]), m_sc[2])
    num = jnp.zeros((seq, HEAD_DIM), F32)
    den = jnp.zeros((seq, HEAD_DIM), F32)
    for bi in range(len(DILATIONS)):
        w = jnp.exp(m_sc[bi] - big_m)
        num = num + w * acc_sc[bi]
        den = den + w * l_sc[bi]
    o_ref[...] = num / den


def _attn_prompt(q, k, v, batch, seq):
    q3, k3, v3 = (a.reshape(batch, seq, D_ATT) for a in (q, k, v))
    head = pl.BlockSpec((None, seq, HEAD_DIM), lambda b, h: (b, 0, h))
    nbr = len(DILATIONS)
    out = pl.pallas_call(
        _attn_prompt_kernel,
        grid=(batch, ATT_HEADS),
        in_specs=[head, head, head],
        out_specs=head,
        out_shape=jax.ShapeDtypeStruct((batch, seq, D_ATT), F32),
        scratch_shapes=[pltpu.VMEM((nbr, seq, HEAD_DIM), F32)] * 3,
        compiler_params=_params("parallel", "parallel"),
        name="attn_prompt",
    )(q3, k3, v3)
    return out.reshape(batch * seq, D_ATT)


def _branch_count(rel):
    cnt = jnp.zeros(rel.shape, F32)
    for d in DILATIONS:
        hit = (rel >= 0) & (rel <= d * BAND) & ((rel & (d - 1)) == 0)
        cnt = cnt + jnp.where(hit, 1.0, 0.0)
    return cnt


SAMPLE_CHUNK = 512
M_FLOOR = -1e30


def _attn_sample_kernel(q_ref, kn_ref, vn_ref, kn3_ref, vn3_ref, ks_ref, ks_next_ref, vs_ref, vs_next_ref,
                        o_ref, ko_ref, vo_ref, m_sc, l_sc, acc_sc, *, n_buf, chunks):
    c = pl.program_id(1)
    rows = ks_ref.shape[0]
    t_new = q_ref.shape[0]
    scale = HEAD_DIM ** -0.5
    log_heads = int(math.log2(ATT_HEADS))
    log_new = int(math.log2(t_new))
    assert ATT_HEADS == 1 << log_heads and t_new == 1 << log_new

    def by_head(ref):
        return jnp.concatenate([ref[:, h * HEAD_DIM:(h + 1) * HEAD_DIM] for h in range(ATT_HEADS)], axis=0)

    @pl.when(c == 0)
    def _():
        m_sc[...] = jnp.full(m_sc.shape, M_FLOOR, F32)
        l_sc[...] = jnp.zeros(l_sc.shape, F32)
        acc_sc[...] = jnp.zeros(acc_sc.shape, F32)

    q = by_head(q_ref).astype(BF16)

    def accumulate(k2, v2, key_head, rel):
        q_head = lax.shift_right_logical(lax.broadcasted_iota(I32, rel.shape, 0), log_new)
        cnt = jnp.where(key_head == q_head, _branch_count(rel), 0.0)
        s = jnp.where(cnt > 0, _dot_t(q, k2.astype(BF16)) * scale, NEG_INF)
        m_old = m_sc[...]
        m_new = jnp.maximum(m_old, jnp.max(s, axis=1, keepdims=True))
        alpha = jnp.exp(m_old - m_new)
        p = cnt * jnp.exp(s - m_new)
        l_sc[...] = alpha * l_sc[...] + jnp.sum(p, axis=1, keepdims=True)
        acc_sc[...] = alpha * acc_sc[...] + jnp.dot(p.astype(BF16), v2.astype(BF16),
                                                    preferred_element_type=F32)
        m_sc[...] = m_new

    ks = ks_ref[...]
    vs = vs_ref[...]
    n = rows * ATT_HEADS
    col = lax.broadcasted_iota(I32, (ATT_HEADS * t_new, n), 1)
    qrow = lax.broadcasted_iota(I32, (ATT_HEADS * t_new, n), 0)
    kpos = c * rows + lax.shift_right_logical(col, log_heads)
    qpos = n_buf + (qrow & (t_new - 1))
    accumulate(ks.reshape(n, HEAD_DIM), vs.reshape(n, HEAD_DIM), col & (ATT_HEADS - 1), qpos - kpos)

    last = c == chunks - 1
    ko_ref[0:rows - t_new] = ks[t_new:]
    vo_ref[0:rows - t_new] = vs[t_new:]
    ko_ref[rows - t_new:rows] = jnp.where(last, kn3_ref[...], ks_next_ref[...])
    vo_ref[rows - t_new:rows] = jnp.where(last, vn3_ref[...], vs_next_ref[...])

    @pl.when(last)
    def _():
        m2 = ATT_HEADS * t_new
        ncol = lax.broadcasted_iota(I32, (m2, m2), 1)
        nrow = lax.broadcasted_iota(I32, (m2, m2), 0)
        accumulate(by_head(kn_ref), by_head(vn_ref), lax.shift_right_logical(ncol, log_new),
                   (nrow & (t_new - 1)) - (ncol & (t_new - 1)))
        out = acc_sc[...] / l_sc[...]
        for h in range(ATT_HEADS):
            o_ref[:, h * HEAD_DIM:(h + 1) * HEAD_DIM] = out[h * t_new:(h + 1) * t_new, :]


def _attn_sample(q, k_new, v_new, k_state, v_state, batch, t_new):
    n_buf = k_state.shape[2]
    rows = min(SAMPLE_CHUNK, n_buf)
    chunks = n_buf // rows
    assert t_new == SUBLANES and rows % t_new == 0
    kn3 = k_new.reshape(batch * t_new, ATT_HEADS, HEAD_DIM)
    vn3 = v_new.reshape(batch * t_new, ATT_HEADS, HEAD_DIM)
    new = pl.BlockSpec((t_new, D_ATT), lambda b, c: (b, 0))
    new3 = pl.BlockSpec((t_new, ATT_HEADS, HEAD_DIM), lambda b, c: (b, 0, 0))
    buf = pl.BlockSpec((None, None, rows, ATT_HEADS, HEAD_DIM), lambda b, c: (0, b, c, 0, 0))
    per = rows // t_new
    nxt = pl.BlockSpec((None, None, t_new, ATT_HEADS, HEAD_DIM),
                       lambda b, c: (0, b, jnp.minimum((c + 1) * per, n_buf // t_new - 1), 0, 0))
    buf_shape = jax.ShapeDtypeStruct(k_state.shape, F32)
    qrows = ATT_HEADS * t_new
    return pl.pallas_call(
        functools.partial(_attn_sample_kernel, n_buf=n_buf, chunks=chunks),
        grid=(batch, chunks),
        in_specs=[new, new, new, new3, new3, buf, nxt, buf, nxt],
        out_specs=[new, buf, buf],
        out_shape=[jax.ShapeDtypeStruct((batch * t_new, D_ATT), F32), buf_shape, buf_shape],
        scratch_shapes=[pltpu.VMEM((qrows, 1), F32), pltpu.VMEM((qrows, 1), F32),
                        pltpu.VMEM((qrows, HEAD_DIM), F32)],
        compiler_params=_params("parallel", "arbitrary"),
        name="attn_sample",
    )(q, k_new, v_new, kn3, vn3, k_state, k_state, v_state, v_state)


def _out_proj_kernel(att_ref, conv_ref, x_ref, w_ref, g_ref, b_ref, o_ref):
    mixed = (jnp.dot(att_ref[...].astype(BF16), w_ref[0:D_ATT, :], preferred_element_type=F32)
             + jnp.dot(conv_ref[...].astype(BF16), w_ref[D_ATT:D_MODEL, :], preferred_element_type=F32))
    o_ref[...] = _layer_norm(DEEPNORM_ALPHA * x_ref[...] + mixed, g_ref[...], b_ref[...])


def _out_proj(att, conv, x, w_bf16, g, b):
    n = x.shape[0]
    tm = min(ROW_TILE, n)
    row = lambda i: (i, 0)
    return pl.pallas_call(
        _out_proj_kernel,
        grid=(n // tm,),
        in_specs=[pl.BlockSpec((tm, D_ATT), row), pl.BlockSpec((tm, D_CONV), row),
                  pl.BlockSpec((tm, D_MODEL), row), _resident(w_bf16.shape),
                  _resident(g.shape), _resident(b.shape)],
        out_specs=pl.BlockSpec((tm, D_MODEL), row),
        out_shape=jax.ShapeDtypeStruct((n, D_MODEL), F32),
        compiler_params=_params("parallel"),
        name="out_proj",
    )(att, conv, x, w_bf16, g, b)


def _out_route_kernel(att_ref, conv_ref, x_ref, w_ref, g_ref, b_ref, wq_ref, sk_ref, o_ref, e_ref, gate_ref):
    _out_proj_kernel(att_ref, conv_ref, x_ref, w_ref, g_ref, b_ref, o_ref)
    _peer_route_kernel(o_ref, wq_ref, sk_ref, e_ref, gate_ref)


def _out_route(att, conv, x, w_bf16, g, b, wq_bf16, subkeys_bf16):
    n = x.shape[0]
    tm = min(ROW_TILE, n)
    row = lambda i: (i, 0)
    return pl.pallas_call(
        _out_route_kernel,
        grid=(n // tm,),
        in_specs=[pl.BlockSpec((tm, D_ATT), row), pl.BlockSpec((tm, D_CONV), row),
                  pl.BlockSpec((tm, D_MODEL), row), _resident(w_bf16.shape),
                  _resident(g.shape), _resident(b.shape),
                  _resident(wq_bf16.shape), _resident(subkeys_bf16.shape)],
        out_specs=[pl.BlockSpec((tm, D_MODEL), row), pl.BlockSpec((tm, PEER_SLOTS), row),
                   pl.BlockSpec((tm, PEER_SLOTS), row)],
        out_shape=[jax.ShapeDtypeStruct((n, D_MODEL), F32),
                   jax.ShapeDtypeStruct((n, PEER_SLOTS), I32),
                   jax.ShapeDtypeStruct((n, PEER_SLOTS), F32)],
        compiler_params=_params("parallel"),
        name="out_route",
    )(att, conv, x, w_bf16, g, b, wq_bf16, subkeys_bf16)


ID_NONE = float(2 ** 24)


def _top_rows(s, ids, k):
    vals, sel = [], []
    for _ in range(k):
        m = jnp.max(s, axis=0, keepdims=True)
        i = jnp.min(jnp.where(s == m, ids, ID_NONE), axis=0, keepdims=True)
        vals.append(m)
        sel.append(i)
        s = jnp.where(ids == i, NEG_INF, s)
    return jnp.concatenate(vals, axis=0), jnp.concatenate(sel, axis=0)


def _pair_candidates(v1, v2, sub):
    assert PEER_TOPK == 2 * SUBLANES
    vals = [v1[0:1, :] + v2]
    ids = [sub, sub + SUBLANES]
    for i in range(1, SUBLANES):
        vals.append(v1[i:i + 1, :] + v2[0:SUBLANES, :])
        ids.append(sub + i * PEER_TOPK)
    vals.append(v1[SUBLANES:, :] + v2[0:1, :])
    ids.append((sub + SUBLANES) * PEER_TOPK)
    return jnp.concatenate(vals, axis=0), jnp.concatenate(ids, axis=0)


def _take_rows(table, idx):
    out = jnp.zeros(idx.shape, table.dtype)
    for i in range(table.shape[0]):
        out = jnp.where(idx == i, table[i:i + 1, :], out)
    return out


def _peer_route_kernel(h_ref, wq_ref, sk_ref, e_ref, g_ref):
    hb = h_ref[...].astype(BF16)
    tokens = hb.shape[0]
    half = PEER_KEY_DIM // 2
    key_ids = lax.broadcasted_iota(I32, (N_SUBKEYS, tokens), 0).astype(F32)
    sub = lax.broadcasted_iota(I32, (SUBLANES, tokens), 0).astype(F32)
    experts, gates = [], []
    for head in range(PEER_HEADS):
        top = []
        for p in range(2):
            c0 = head * PEER_KEY_DIM + p * half
            qhp = jnp.dot(hb, wq_ref[:, c0:c0 + half], preferred_element_type=F32)
            sc = _dot_t(sk_ref[p], qhp.astype(BF16))
            top.append(_top_rows(sc, key_ids, PEER_TOPK))
        (v1, i1), (v2, i2) = top
        cand, cand_ids = _pair_candidates(v1, v2, sub)
        c_top, c_id = _top_rows(cand, cand_ids, PEER_TOPK)
        rank1 = jnp.floor(c_id * (1.0 / PEER_TOPK))
        rank2 = c_id - rank1 * PEER_TOPK
        experts.append(_take_rows(i1, rank1) * N_SUBKEYS + _take_rows(i2, rank2))
        ex = jnp.exp(c_top - jnp.max(c_top, axis=0, keepdims=True))
        gates.append(ex / jnp.sum(ex, axis=0, keepdims=True))
    e_ref[...] = jnp.concatenate(experts, axis=0).T.astype(I32)
    g_ref[...] = jnp.concatenate(gates, axis=0).T


def _peer_route(h, wq_bf16, subkeys_bf16):
    n = h.shape[0]
    tm = min(ROW_TILE, n)
    row = lambda i: (i, 0)
    return pl.pallas_call(
        _peer_route_kernel,
        grid=(n // tm,),
        in_specs=[pl.BlockSpec((tm, D_MODEL), row), _resident(wq_bf16.shape),
                  _resident(subkeys_bf16.shape)],
        out_specs=[pl.BlockSpec((tm, PEER_SLOTS), row)] * 2,
        out_shape=[jax.ShapeDtypeStruct((n, PEER_SLOTS), I32),
                   jax.ShapeDtypeStruct((n, PEER_SLOTS), F32)],
        compiler_params=_params("parallel"),
        name="peer_route",
    )(h, wq_bf16, subkeys_bf16)


PEER_TOKENS = 64
PEER_RING = 8
PIECE_GROUPS = 2


def _fold(row):
    return jnp.concatenate([row[:, j * LANES:(j + 1) * LANES] for j in range(FOLD)], axis=0)


def _unfold(folded):
    return jnp.concatenate([folded[j:j + 1, :] for j in range(FOLD)], axis=1)


def _sublane_sums(vregs):
    sub = lax.broadcasted_iota(I32, (SUBLANES, LANES), 0)
    roll = lambda a, k: pltpu.roll(a, k, axis=0)
    p = [vregs[i] for i in (3, 2, 1, 0, 7, 6, 5, 4)]
    m = [jnp.where(sub < 4, p[i] + roll(p[i], 4), p[i + 4] + roll(p[i + 4], 4)) for i in range(4)]
    n = [jnp.where((sub & 3) >= 2, m[i] + roll(m[i], 2), roll(m[i + 2] + roll(m[i + 2], 2), 6))
         for i in range(2)]
    return jnp.where((sub & 1) == 1, n[0] + roll(n[0], 1), roll(n[1] + roll(n[1], 1), 7))


def _peer_expert_kernel(e_ref, e_next_ref, h_ref, g_ref, uv_hbm, o_ref, buf, sem, *, steps):
    tokens = h_ref.shape[0]
    ahead = PEER_RING - 1
    assert tokens % PEER_RING == 0 and tokens > ahead and FOLD == 2 * SUBLANES
    step = pl.program_id(0)
    eye = (lax.broadcasted_iota(I32, (PEER_SLOTS, PEER_SLOTS), 0)
           == lax.broadcasted_iota(I32, (PEER_SLOTS, PEER_SLOTS), 1))

    groups = PEER_SLOTS // SUBLANES
    pieces = 2 * groups // PIECE_GROUPS
    per_piece = PEER_SLOTS // pieces

    def start_piece(ids_ref, t, slot, piece):
        for k in range(piece * per_piece, (piece + 1) * per_piece):
            pltpu.make_async_copy(uv_hbm.at[ids_ref[t, k]], buf.at[slot, k], sem.at[slot]).start(priority=k % 2)

    def wait(slot):
        pltpu.make_async_copy(uv_hbm.at[pl.ds(0, PEER_SLOTS)], buf.at[slot], sem.at[slot]).wait()

    def token(ids_ref, t_ahead, slot_ahead, t, slot):
        wait(slot)
        x = _fold(h_ref[pl.ds(t, 1), :])
        sums = []
        for c in range(groups):
            if c % PIECE_GROUPS == 0:
                start_piece(ids_ref, t_ahead, slot_ahead, c // PIECE_GROUPS)
            rows = slice(c * SUBLANES, (c + 1) * SUBLANES)
            prod = buf[slot, rows, 0:FOLD, :].astype(F32) * x[None]
            sums.append(_sublane_sums([prod[i, 0:SUBLANES, :] + prod[i, SUBLANES:FOLD, :]
                                       for i in range(SUBLANES)]))
        hid = jnp.sum(jnp.concatenate(sums, axis=0), axis=1, keepdims=True)
        gate = jnp.sum(jnp.where(eye, g_ref[pl.ds(t, 1), :], 0.0), axis=1, keepdims=True)
        a = gate * (0.5 * hid * (1.0 + lax.erf(hid * (2.0 ** -0.5))))
        a = jnp.broadcast_to(a, (PEER_SLOTS, LANES))
        y = jnp.zeros((FOLD, LANES), F32)
        for c in range(groups):
            if c % PIECE_GROUPS == 0:
                start_piece(ids_ref, t_ahead, slot_ahead, pieces // 2 + c // PIECE_GROUPS)
            rows = slice(c * SUBLANES, (c + 1) * SUBLANES)
            y = y + jnp.sum(a[rows][:, None, :] * buf[slot, rows, FOLD:2 * FOLD, :].astype(F32), axis=0)
        o_ref[pl.ds(t, 1), :] = _unfold(y)

    @pl.when(step == 0)
    def _():
        for t in range(ahead):
            for piece in range(pieces):
                start_piece(e_ref, t, t, piece)

    def body(t, carry):
        token(e_ref, t + ahead, (t + ahead) % PEER_RING, t, t % PEER_RING)
        return carry

    lax.fori_loop(0, tokens - ahead, body, 0)
    for j in range(ahead):
        t = tokens - ahead + j
        token(e_next_ref, j, j % PEER_RING, t, t % PEER_RING)

    @pl.when(step == steps - 1)
    def _():
        for j in range(ahead):
            wait(j % PEER_RING)


def _peer_expert(experts, h, gates, uv_tab):
    n = h.shape[0]
    tb = min(PEER_TOKENS, n)
    steps = n // tb
    row = lambda i: (i, 0)
    return pl.pallas_call(
        functools.partial(_peer_expert_kernel, steps=steps),
        grid=(steps,),
        in_specs=[pl.BlockSpec((tb, PEER_SLOTS), row, memory_space=pltpu.SMEM),
                  pl.BlockSpec((tb, PEER_SLOTS), lambda i: (jnp.minimum(i + 1, steps - 1), 0),
                               memory_space=pltpu.SMEM),
                  pl.BlockSpec((tb, D_MODEL), row),
                  pl.BlockSpec((tb, PEER_SLOTS), row),
                  pl.BlockSpec(memory_space=pl.ANY)],
        out_specs=pl.BlockSpec((tb, D_MODEL), row),
        out_shape=jax.ShapeDtypeStruct((n, D_MODEL), F32),
        scratch_shapes=[pltpu.VMEM((PEER_RING, PEER_SLOTS, 2 * FOLD, LANES), uv_tab.dtype),
                        pltpu.SemaphoreType.DMA((PEER_RING,))],
        compiler_params=_params("arbitrary"),
        name="peer_expert",
    )(experts, experts, h, gates, uv_tab)


def _tail_kernel(h_ref, y_ref, p_ref, g_ref, b_ref, wg_ref, wp_ref, o_ref):
    h2 = _layer_norm(DEEPNORM_ALPHA * h_ref[...] + y_ref[...], g_ref[...], b_ref[...])
    gate = jax.nn.sigmoid(jnp.dot(h2.astype(BF16), wg_ref[...], preferred_element_type=F32))
    emb = jnp.dot(p_ref[...].astype(BF16), wp_ref[...], preferred_element_type=F32)
    o_ref[...] = h2 + gate * emb


def _tail(h, y, p, g, b, wg_bf16, wp_bf16):
    n = h.shape[0]
    tm = min(ROW_TILE, n)
    row = lambda i: (i, 0)
    return pl.pallas_call(
        _tail_kernel,
        grid=(n // tm,),
        in_specs=[pl.BlockSpec((tm, D_MODEL), row), pl.BlockSpec((tm, D_MODEL), row),
                  pl.BlockSpec((tm, p.shape[1]), row), _resident(g.shape), _resident(b.shape),
                  _resident(wg_bf16.shape), _resident(wp_bf16.shape)],
        out_specs=pl.BlockSpec((tm, D_MODEL), row),
        out_shape=jax.ShapeDtypeStruct((n, D_MODEL), F32),
        compiler_params=_params("parallel"),
        name="tail",
    )(h, y, p, g, b, wg_bf16, wp_bf16)


def _rotary_tables(pos):
    half = HEAD_DIM // 2
    inv = ROPE_THETA ** (-jnp.arange(half, dtype=F32) / half)
    ang = pos.astype(F32)[:, None] * inv[None, :]
    cos, sin = jnp.cos(ang), jnp.sin(ang)
    return jnp.concatenate([cos, cos], axis=1), jnp.concatenate([-sin, sin], axis=1)


def _state_halo(state):
    return jnp.pad(state, ((0, 0), (SUBLANES - (CONV_WIDTH - 1), 0), (0, 0)))


def _layer_tail(x, att, conv, p, w):
    h1, experts, gates = _out_route(att, conv, x, w["w_out"], w["ln1_g"], w["ln1_b"],
                                    w["peer_wq"], w["peer_subkeys"])
    y = _peer_expert(experts, h1, gates, w["peer_uv"])
    return _tail(h1, y, p, w["ln2_g"], w["ln2_b"], w["ple_gate"], w["ple_proj"])


def kernel(x_prompt, x_sample, state_attn_k, state_attn_v, state_conv, p_prompt, p_sample,
           w_in, w_conv, w_out, ln1_g, ln1_b, peer_wq, peer_subkeys, peer_u, peer_v,
           ln2_g, ln2_b, ple_gate, ple_proj):
    assert w_in.shape[0] == DEPTH
    bp, tp, _ = x_prompt.shape
    bs, ts, _ = x_sample.shape
    w = {
        "w_in": w_in[0].astype(BF16), "w_out": w_out[0].astype(BF16),
        "ln1_g": ln1_g, "ln1_b": ln1_b, "ln2_g": ln2_g, "ln2_b": ln2_b,
        "peer_wq": peer_wq[0].astype(BF16), "peer_subkeys": peer_subkeys[0].astype(BF16),
        "peer_uv": jnp.concatenate([peer_u.reshape(-1, FOLD, LANES), peer_v.reshape(-1, FOLD, LANES)],
                                   axis=1).astype(BF16),
        "ple_gate": ple_gate[0].astype(BF16), "ple_proj": ple_proj[0].astype(BF16),
    }
    wc = w_conv[0]

    xp = x_prompt.reshape(bp * tp, D_MODEL)
    cos_p, sin_p = _rotary_tables(jnp.arange(tp))
    q, k, v, z, gb = _in_proj(xp, w["w_in"], cos_p, sin_p)
    conv = _conv(z, gb, jnp.zeros((bp, SUBLANES, D_CONV), F32), wc, bp, tp)
    att = _attn_prompt(q, k, v, bp, tp)
    y_prompt = _layer_tail(xp, att, conv, p_prompt[0].reshape(bp * tp, -1), w).reshape(bp, tp, D_MODEL)
    new_k_prompt = k.reshape(1, bp, tp, ATT_HEADS, HEAD_DIM)
    new_v_prompt = v.reshape(1, bp, tp, ATT_HEADS, HEAD_DIM)
    new_conv_prompt = z.reshape(bp, tp, D_CONV)[:, tp - (CONV_WIDTH - 1):][None]

    xs = x_sample.reshape(bs * ts, D_MODEL)
    cos_s, sin_s = _rotary_tables(jnp.tile(PAST_LEN + jnp.arange(ts), bs))
    q, k, v, z, gb = _in_proj(xs, w["w_in"], cos_s, sin_s)
    conv = _conv(z, gb, _state_halo(state_conv[0]), wc, bs, ts)
    att, new_k_sample, new_v_sample = _attn_sample(q, k, v, state_attn_k, state_attn_v, bs, ts)
    y_sample = _layer_tail(xs, att, conv, p_sample[0].reshape(bs * ts, -1), w).reshape(bs, ts, D_MODEL)
    z_ext = jnp.concatenate([state_conv[0], z.reshape(bs, ts, D_CONV)], axis=1)
    new_conv_sample = z_ext[:, -(CONV_WIDTH - 1):][None]

    return (y_prompt, y_sample, new_k_prompt, new_v_prompt, new_conv_prompt,
            new_k_sample, new_v_sample, new_conv_sample)
```
